```python
import jax, jax.numpy as jnp
from jax import lax
import numpy as np

D_MODEL = 2048
BATCH = 2
SEQ = 4096
DEPTH = 4
DEC_BATCH = 8
DEC_SEQ = 8
PAST_LEN = 16384
PAGE_SIZE = 128

N_CB = (DEPTH + 1) // 2
N_SB = DEPTH // 2
D_A = D_MODEL // 2
D_B = D_MODEL // 2
CONV_W = 31
POOL_WINDOWS = (2, 4, 8, 16)
N_POOL_GROUPS = len(POOL_WINDOWS)
POOL_GROUP = D_B // N_POOL_GROUPS
MAX_WIN = 16
N_HEADS = 16
HEAD_DIM = D_MODEL // N_HEADS
Q_BLOCK = 128
D_FF = 5632
FFN_CONV_W = 3
RMS_EPS = 1e-6
LN_EPS = 1e-5
SB_BIAS_NEAR = -3.0
SB_BIAS_FAR = -10.0

kernel_name = "hybrid_conv_pool_stickbreak_decoder_step"


def rmsnorm(x, g):
    xf = x.astype(jnp.float32)
    y = xf * lax.rsqrt(jnp.mean(xf * xf, axis=-1, keepdims=True) + RMS_EPS)
    return (y * g.astype(jnp.float32)).astype(x.dtype)


def layernorm(x, g, b):
    xf = x.astype(jnp.float32)
    mu = jnp.mean(xf, axis=-1, keepdims=True)
    var = jnp.mean(jnp.square(xf - mu), axis=-1, keepdims=True)
    y = (xf - mu) * lax.rsqrt(var + LN_EPS) * g.astype(jnp.float32) + b.astype(jnp.float32)
    return y.astype(x.dtype)


def causal_dwconv(ext, w, b):
    c = ext.shape[-1]
    out = lax.conv_general_dilated(
        ext, w[:, None, :].astype(ext.dtype), window_strides=(1,), padding='VALID',
        dimension_numbers=('NWC', 'WIO', 'NWC'), feature_group_count=c)
    return out + b.astype(ext.dtype)


def pool_mix(ext, start_pos, pool_w, pool_scale):
    xf = ext.astype(jnp.float32)
    bsz, length, _ = xf.shape
    t = length - (MAX_WIN - 1)
    cs = jnp.concatenate([jnp.zeros((bsz, 1, D_B), jnp.float32), jnp.cumsum(xf, axis=1)], axis=1)
    hi = cs[:, MAX_WIN:]
    x_tok = xf[:, MAX_WIN - 1:]
    pos = start_pos + jnp.arange(t)
    diffs = []
    for g, w in enumerate(POOL_WINDOWS):
        sl = slice(g * POOL_GROUP, (g + 1) * POOL_GROUP)
        lo = cs[:, MAX_WIN - w: MAX_WIN - w + t, sl]
        cnt = jnp.minimum(w, pos + 1).astype(jnp.float32)[None, :, None]
        diffs.append((hi[..., sl] - lo) / cnt - x_tok[..., sl])
    d = jnp.stack(diffs, axis=2)
    y = jnp.einsum('btgc,gcd->btgd', d, pool_w.astype(jnp.float32)).reshape(bsz, t, D_B)
    return (y * pool_scale.astype(jnp.float32)).astype(ext.dtype)


def conv_pool_mixer(h, conv_prev, pool_prev, start_pos, w_in, cw, cb, lg, lb, pw, ps, w_out):
    p = h @ w_in
    a_val, a_gate, b_in = jnp.split(p, [D_A, 2 * D_A], axis=-1)
    glu = a_val * jax.nn.sigmoid(a_gate)
    ext_a = jnp.concatenate([conv_prev.astype(glu.dtype), glu], axis=1)
    a = jax.nn.silu(layernorm(causal_dwconv(ext_a, cw, cb), lg, lb))
    ext_b = jnp.concatenate([pool_prev.astype(b_in.dtype), b_in], axis=1)
    b = pool_mix(ext_b, start_pos, pw, ps)
    out = jnp.concatenate([a, b], axis=-1) @ w_out
    return out, ext_a[:, -(CONV_W - 1):], ext_b[:, -(MAX_WIN - 1):]


def stickbreak_block(q, k, v, q_pos, k_pos, bias):
    z = jnp.einsum('bqhd,bkhd->bhqk', q, k, preferred_element_type=jnp.float32) * (HEAD_DIM ** -0.5)
    z = z + bias.astype(jnp.float32)[None, :, None, None]
    mask = k_pos[None, :] < q_pos[:, None]
    log_stay = jnp.where(mask, jax.nn.log_sigmoid(-z), 0.0)
    rest = lax.cumsum(log_stay, axis=3, reverse=True) - log_stay
    w = jnp.where(mask, jnp.exp(jax.nn.log_sigmoid(z) + rest), 0.0)
    return jnp.einsum('bhqk,bkhd->bqhd', w.astype(v.dtype), v)


def stickbreak_sweep(q, k, v, q_start, bias):
    bsz, t = q.shape[0], q.shape[1]
    k_pos = jnp.arange(k.shape[1])
    if t <= Q_BLOCK:
        return stickbreak_block(q, k, v, q_start + jnp.arange(t), k_pos, bias)
    nb = t // Q_BLOCK
    qb = q.reshape(bsz, nb, Q_BLOCK, N_HEADS, HEAD_DIM).transpose(1, 0, 2, 3, 4)
    pos = (q_start + jnp.arange(t)).reshape(nb, Q_BLOCK)
    ob = lax.map(lambda a: stickbreak_block(a[0], k, v, a[1], k_pos, bias), (qb, pos))
    return ob.transpose(1, 0, 2, 3, 4).reshape(bsz, t, N_HEADS, HEAD_DIM)


def sb_mixer(h, k_past, v_past, start_pos, w_qkv, w_o, bias):
    bsz, t, _ = h.shape
    q, k, v = jnp.split(h @ w_qkv, 3, axis=-1)
    q = q.reshape(bsz, t, N_HEADS, HEAD_DIM)
    k = k.reshape(bsz, t, N_HEADS, HEAD_DIM)
    v = v.reshape(bsz, t, N_HEADS, HEAD_DIM)
    if k_past is None:
        k_all, v_all = k, v
    else:
        k_all = jnp.concatenate([k_past.astype(k.dtype), k], axis=1)
        v_all = jnp.concatenate([v_past.astype(v.dtype), v], axis=1)
    o = stickbreak_sweep(q, k_all, v_all, start_pos, bias)
    return o.reshape(bsz, t, D_MODEL) @ w_o, k, v


def conv_ffn(h, prev, w_up, cw, cb, w_down):
    a, g = jnp.split(h @ w_up, 2, axis=-1)
    ext = jnp.concatenate([prev.astype(a.dtype), a], axis=1)
    y = (jax.nn.gelu(causal_dwconv(ext, cw, cb)) * g) @ w_down
    return y, ext[:, -(FFN_CONV_W - 1):]


def trunk(x, start_pos, conv_prev, pool_prev, ffn_prev, cache_k, cache_v, page_table,
          norm_mix, norm_ffn, norm_final, w_in_cb, conv_w, conv_b, ln_a_g, ln_a_b,
          pool_w, pool_scale, w_out_cb, w_qkv, w_o_sb, sb_bias, w_up, ffn_conv_w, ffn_conv_b, w_down):
    bsz = x.shape[0]
    convs, pools, ks, vs, ffns = [], [], [], [], []
    for layer in range(DEPTH):
        i = layer // 2
        h = rmsnorm(x, norm_mix[layer])
        if layer % 2 == 0:
            out, c_new, p_new = conv_pool_mixer(h, conv_prev[i], pool_prev[i], start_pos,
                                                w_in_cb[i], conv_w[i], conv_b[i], ln_a_g[i], ln_a_b[i],
                                                pool_w[i], pool_scale[i], w_out_cb[i])
            convs.append(c_new)
            pools.append(p_new)
        else:
            if page_table is None:
                k_past, v_past = None, None
            else:
                k_past = cache_k[i][page_table].reshape(bsz, -1, N_HEADS, HEAD_DIM)
                v_past = cache_v[i][page_table].reshape(bsz, -1, N_HEADS, HEAD_DIM)
            out, k_new, v_new = sb_mixer(h, k_past, v_past, start_pos, w_qkv[i], w_o_sb[i], sb_bias[i])
            ks.append(k_new)
            vs.append(v_new)
        x = x + out
        f, s_new = conv_ffn(rmsnorm(x, norm_ffn[layer]), ffn_prev[layer],
                            w_up[layer], ffn_conv_w[layer], ffn_conv_b[layer], w_down[layer])
        x = x + f
        ffns.append(s_new)
    return (rmsnorm(x, norm_final), jnp.stack(convs), jnp.stack(pools),
            jnp.stack(ks), jnp.stack(vs), jnp.stack(ffns))


def setup_inputs(seed: int = 0) -> dict:
    key = jax.random.key(seed)
    ks = jax.random.split(key, 32)
    f32 = jnp.float32

    def nrm(k, shape, scale):
        return jax.random.normal(k, shape, f32) * scale

    n_pages = PAST_LEN // PAGE_SIZE
    n_pool = (5 * DEC_BATCH * n_pages) // 4
    page_table = jax.random.permutation(ks[0], n_pool)[:DEC_BATCH * n_pages]
    page_table = page_table.reshape(DEC_BATCH, n_pages).astype(jnp.int32)
    sb_bias = jnp.linspace(SB_BIAS_NEAR, SB_BIAS_FAR, N_HEADS, dtype=f32)[None, :] + nrm(ks[25], (N_SB, N_HEADS), 0.1)
    return {
        "x_prompt": nrm(ks[1], (BATCH, SEQ, D_MODEL), 1.0),
        "x_sample": nrm(ks[2], (DEC_BATCH, DEC_SEQ, D_MODEL), 1.0),
        "state_conv": nrm(ks[3], (N_CB, DEC_BATCH, CONV_W - 1, D_A), 0.5),
        "state_pool": nrm(ks[4], (N_CB, DEC_BATCH, MAX_WIN - 1, D_B), 1.0),
        "cache_k": nrm(ks[5], (N_SB, n_pool, PAGE_SIZE, N_HEADS, HEAD_DIM), 1.0),
        "cache_v": nrm(ks[6], (N_SB, n_pool, PAGE_SIZE, N_HEADS, HEAD_DIM), 1.0),
        "page_table": page_table,
        "state_ffn": nrm(ks[7], (DEPTH, DEC_BATCH, FFN_CONV_W - 1, D_FF), 1.0),
        "norm_mix": 1.0 + nrm(ks[8], (DEPTH, D_MODEL), 0.01),
        "norm_ffn": 1.0 + nrm(ks[9], (DEPTH, D_MODEL), 0.01),
        "norm_final": 1.0 + nrm(ks[10], (D_MODEL,), 0.01),
        "w_in_cb": nrm(ks[11], (N_CB, D_MODEL, 2 * D_A + D_B), D_MODEL ** -0.5),
        "conv_w": nrm(ks[12], (N_CB, CONV_W, D_A), CONV_W ** -0.5),
        "conv_b": nrm(ks[13], (N_CB, D_A), 0.01),
        "ln_a_g": 1.0 + nrm(ks[14], (N_CB, D_A), 0.01),
        "ln_a_b": nrm(ks[15], (N_CB, D_A), 0.01),
        "pool_w": nrm(ks[16], (N_CB, N_POOL_GROUPS, POOL_GROUP, POOL_GROUP), POOL_GROUP ** -0.5),
        "pool_scale": 1.0 + nrm(ks[17], (N_CB, D_B), 0.01),
        "w_out_cb": nrm(ks[18], (N_CB, D_A + D_B, D_MODEL), (D_A + D_B) ** -0.5),
        "w_qkv": nrm(ks[19], (N_SB, D_MODEL, 3 * D_MODEL), D_MODEL ** -0.5),
        "w_o_sb": nrm(ks[20], (N_SB, D_MODEL, D_MODEL), D_MODEL ** -0.5),
        "sb_bias": sb_bias,
        "w_up": nrm(ks[21], (DEPTH, D_MODEL, 2 * D_FF), D_MODEL ** -0.5),
        "ffn_conv_w": nrm(ks[22], (DEPTH, FFN_CONV_W, D_FF), FFN_CONV_W ** -0.5),
        "ffn_conv_b": nrm(ks[23], (DEPTH, D_FF), 0.01),
        "w_down": nrm(ks[24], (DEPTH, D_FF, D_MODEL), D_FF ** -0.5),
    }


def reference(x_prompt, x_sample, state_conv, state_pool, cache_k, cache_v, page_table, state_ffn,
              norm_mix, norm_ffn, norm_final, w_in_cb, conv_w, conv_b, ln_a_g, ln_a_b,
              pool_w, pool_scale, w_out_cb, w_qkv, w_o_sb, sb_bias, w_up, ffn_conv_w, ffn_conv_b, w_down):
    bp = x_prompt.shape[0]
    dt = x_prompt.dtype
    zero_conv = jnp.zeros((N_CB, bp, CONV_W - 1, D_A), dt)
    zero_pool = jnp.zeros((N_CB, bp, MAX_WIN - 1, D_B), dt)
    zero_ffn = jnp.zeros((DEPTH, bp, FFN_CONV_W - 1, D_FF), dt)
    y_prompt, conv_p, pool_p, k_p, v_p, ffn_p = trunk(
        x_prompt, 0, zero_conv, zero_pool, zero_ffn, None, None, None,
        norm_mix, norm_ffn, norm_final, w_in_cb, conv_w, conv_b, ln_a_g, ln_a_b,
        pool_w, pool_scale, w_out_cb, w_qkv, w_o_sb, sb_bias, w_up, ffn_conv_w, ffn_conv_b, w_down)
    past_len = page_table.shape[1] * PAGE_SIZE
    y_sample, conv_s, pool_s, k_s, v_s, ffn_s = trunk(
        x_sample, past_len, state_conv, state_pool, state_ffn, cache_k, cache_v, page_table,
        norm_mix, norm_ffn, norm_final, w_in_cb, conv_w, conv_b, ln_a_g, ln_a_b,
        pool_w, pool_scale, w_out_cb, w_qkv, w_o_sb, sb_bias, w_up, ffn_conv_w, ffn_conv_b, w_down)
    return (y_prompt, y_sample, conv_p, pool_p, k_p, v_p, ffn_p, conv_s, pool_s, k_s, v_s, ffn_s)
```

```python
import functools

import jax
import jax.numpy as jnp
from jax import lax
from jax.experimental import pallas as pl
from jax.experimental.pallas import tpu as pltpu

F32 = jnp.float32
BF16 = jnp.bfloat16

D_MODEL = 2048
DEPTH = 4
PAGE_SIZE = 128
D_A = D_MODEL // 2
D_B = D_MODEL // 2
CONV_W = 31
POOL_WINDOWS = (2, 4, 8, 16)
POOL_GROUP = D_B // len(POOL_WINDOWS)
MAX_WIN = 16
N_HEADS = 16
HEAD_DIM = D_MODEL // N_HEADS
D_FF = 5632
FFN_CONV_W = 3
RMS_EPS = 1e-6
LN_EPS = 1e-5

V7X_VMEM_LIMIT_BYTES = 56 * 1024 * 1024
CONV_HALO = 32
POOL_HALO = 16
KEY_SUB = 256


def _params(sem):
    return pltpu.CompilerParams(dimension_semantics=sem, vmem_limit_bytes=V7X_VMEM_LIMIT_BYTES)


def _rms(x, g):
    ms = jnp.mean(x * x, axis=-1, keepdims=True)
    return x * lax.rsqrt(ms + RMS_EPS) * g


def _norm_proj_kernel(x_ref, g_ref, *refs, n_w, epilogue):
    w_refs = refs[:n_w]
    o_refs = refs[n_w:-1]
    h_scr = refs[-1]

    @pl.when(pl.program_id(1) == 0)
    def _():
        h_scr[...] = _rms(x_ref[...], g_ref[...]).astype(BF16)

    h = h_scr[...]
    outs = epilogue(*[jnp.dot(h, w[...], preferred_element_type=F32) for w in w_refs])
    for o_ref, v in zip(o_refs, outs):
        o_ref[...] = v.astype(o_ref.dtype)


def _glu_epilogue(a_val, a_gate, b_in):
    return a_val * jax.nn.sigmoid(a_gate), b_in


def _qkv_epilogue(q, k, v):
    return q * (HEAD_DIM ** -0.5), k, v


def norm_proj(x2d, g, w_all, layer, n_w, n_each, out_dtypes, epilogue, tm, tn):
    m = x2d.shape[0]
    nj = n_each // tn
    in_specs = [pl.BlockSpec((tm, D_MODEL), lambda i, j: (i, 0)),
                pl.BlockSpec((1, D_MODEL), lambda i, j: (0, 0))]
    for k in range(n_w):
        in_specs.append(pl.BlockSpec((None, D_MODEL, tn), lambda i, j, k=k: (layer, 0, j + k * nj)))
    return pl.pallas_call(
        functools.partial(_norm_proj_kernel, n_w=n_w, epilogue=epilogue),
        grid=(m // tm, nj),
        in_specs=in_specs,
        out_specs=[pl.BlockSpec((tm, tn), lambda i, j: (i, j)) for _ in out_dtypes],
        out_shape=[jax.ShapeDtypeStruct((m, n_each), dt) for dt in out_dtypes],
        scratch_shapes=[pltpu.VMEM((tm, D_MODEL), BF16)],
        compiler_params=_params(("parallel", "arbitrary")),
    )(x2d, g, *([w_all] * n_w))


def _proj_res_kernel(a_ref, w_ref, r_ref, o_ref):
    o_ref[...] = r_ref[...] + jnp.dot(a_ref[...].astype(BF16), w_ref[...], preferred_element_type=F32)


def proj_res(a2d, w_all, layer, res2d, tm, tn):
    m, kdim = a2d.shape
    n = res2d.shape[1]
    return pl.pallas_call(
        _proj_res_kernel,
        grid=(m // tm, n // tn),
        in_specs=[pl.BlockSpec((tm, kdim), lambda i, j: (i, 0)),
                  pl.BlockSpec((None, kdim, tn), lambda i, j: (layer, 0, j)),
                  pl.BlockSpec((tm, tn), lambda i, j: (i, j))],
        out_specs=pl.BlockSpec((tm, tn), lambda i, j: (i, j)),
        out_shape=jax.ShapeDtypeStruct((m, n), F32),
        compiler_params=_params(("parallel", "arbitrary")),
    )(a2d, w_all, res2d)


def _strict_lower_ones(n):
    r = lax.broadcasted_iota(jnp.int32, (n, n), 0)
    c = lax.broadcasted_iota(jnp.int32, (n, n), 1)
    return jnp.where(r > c, 1.0, 0.0).astype(BF16)


def _sb_weights(z, carry, tri, mask):
    soft = jnp.log(1.0 + jnp.exp(-jnp.abs(z)))
    ls = jnp.minimum(-z, 0.0) - soft
    lsig = z + ls
    if mask is not None:
        ls = jnp.where(mask, ls, 0.0)
    hi = ls.astype(BF16)
    lo = (ls - hi.astype(F32)).astype(BF16)
    rest = (jnp.dot(hi, tri, preferred_element_type=F32)
            + jnp.dot(lo, tri, preferred_element_type=F32))
    w = jnp.exp(lsig + rest + carry)
    if mask is not None:
        w = jnp.where(mask, w, 0.0)
    return w, carry + jnp.sum(ls, axis=-1, keepdims=True)


def _sb_prompt_kernel(bias_ref, q_ref, k_ref, v_ref, o_ref, kb_scr, vb_scr, carry_scr, acc_scr, *, tq):
    h = pl.program_id(1)
    qi = pl.program_id(2)

    @pl.when(qi == 0)
    def _():
        kb_scr[...] = k_ref[...].astype(BF16)
        vb_scr[...] = v_ref[...].astype(BF16)

    q = q_ref[...]
    bias = bias_ref[h]
    tri = _strict_lower_ones(KEY_SUB)
    carry_scr[...] = jnp.zeros_like(carry_scr)
    acc_scr[...] = jnp.zeros_like(acc_scr)
    n_diag = tq // KEY_SUB

    def block(kb, masked):
        start = pl.multiple_of(kb * KEY_SUB, KEY_SUB)
        ks = kb_scr[pl.ds(start, KEY_SUB), :]
        vs = vb_scr[pl.ds(start, KEY_SUB), :]
        z = lax.dot_general(q, ks, (((1,), (1,)), ((), ())), preferred_element_type=F32) + bias
        mask = None
        if masked:
            q_pos = qi * tq + lax.broadcasted_iota(jnp.int32, (tq, KEY_SUB), 0)
            k_pos = kb * KEY_SUB + lax.broadcasted_iota(jnp.int32, (tq, KEY_SUB), 1)
            mask = k_pos < q_pos
        w, carry = _sb_weights(z, carry_scr[...], tri, mask)
        carry_scr[...] = carry
        acc_scr[...] += jnp.dot(w.astype(BF16), vs, preferred_element_type=F32)

    for d in range(n_diag):
        block(qi * n_diag + (n_diag - 1 - d), True)

    def body(j, c):
        block(qi * n_diag - 1 - j, False)
        return c

    lax.fori_loop(0, qi * n_diag, body, 0)
    o_ref[...] = acc_scr[...].astype(o_ref.dtype)


def sb_prompt_attention(q2d, k2d, v2d, bias, bsz, t, tq):
    nq = t // tq
    return pl.pallas_call(
        functools.partial(_sb_prompt_kernel, tq=tq),
        grid=(bsz, N_HEADS, nq),
        in_specs=[pl.BlockSpec(memory_space=pltpu.SMEM),
                  pl.BlockSpec((tq, HEAD_DIM), lambda b, h, i: (b * nq + i, h)),
                  pl.BlockSpec((t, HEAD_DIM), lambda b, h, i: (b, h)),
                  pl.BlockSpec((t, HEAD_DIM), lambda b, h, i: (b, h))],
        out_specs=pl.BlockSpec((tq, HEAD_DIM), lambda b, h, i: (b * nq + i, h)),
        out_shape=jax.ShapeDtypeStruct((bsz * t, D_MODEL), BF16),
        scratch_shapes=[pltpu.VMEM((t, HEAD_DIM), BF16), pltpu.VMEM((t, HEAD_DIM), BF16),
                        pltpu.VMEM((tq, 1), F32), pltpu.VMEM((tq, HEAD_DIM), F32)],
        compiler_params=_params(("parallel", "parallel", "arbitrary")),
    )(bias, q2d, k2d, v2d)


def _sb_sample_kernel(pt_ref, q_ref, kn_ref, vn_ref, brow_ref, *refs, pages_per_step, t_new):
    k_refs = refs[:pages_per_step]
    v_refs = refs[pages_per_step:2 * pages_per_step]
    o_ref = refs[2 * pages_per_step]
    carry_scr, acc_scr = refs[2 * pages_per_step + 1:]
    s = pl.program_id(1)
    rows = N_HEADS * t_new
    tri = _strict_lower_ones(PAGE_SIZE)
    brow = brow_ref[...]

    def logits(key_of_head):
        parts = []
        for h in range(N_HEADS):
            qh = q_ref[:, h * HEAD_DIM:(h + 1) * HEAD_DIM].astype(BF16)
            parts.append(lax.dot_general(qh, key_of_head(h), (((1,), (1,)), ((), ())),
                                         preferred_element_type=F32))
        return jnp.concatenate(parts, axis=0) + brow

    def accumulate(w, value_of_head):
        for h in range(N_HEADS):
            wh = w[h * t_new:(h + 1) * t_new, :].astype(BF16)
            oh = jnp.dot(wh, value_of_head(h), preferred_element_type=F32)
            acc_scr[:, h * HEAD_DIM:(h + 1) * HEAD_DIM] += oh

    @pl.when(s == 0)
    def _():
        acc_scr[...] = jnp.zeros_like(acc_scr)
        z = logits(lambda h: kn_ref[:, h * HEAD_DIM:(h + 1) * HEAD_DIM].astype(BF16))
        r = lax.broadcasted_iota(jnp.int32, (rows, PAGE_SIZE), 0)
        c = lax.broadcasted_iota(jnp.int32, (rows, PAGE_SIZE), 1)
        mask = c < (r % t_new)
        w, carry = _sb_weights(z, jnp.zeros((rows, 1), F32), tri, mask)
        carry_scr[...] = carry
        accumulate(w, lambda h: vn_ref[:, h * HEAD_DIM:(h + 1) * HEAD_DIM].astype(BF16))

    @pl.when(s > 0)
    def _():
        for p in range(pages_per_step):
            k_ref, v_ref = k_refs[p], v_refs[p]
            z = logits(lambda h: k_ref[pl.ds(h, PAGE_SIZE, stride=N_HEADS), :].astype(BF16))
            w, carry = _sb_weights(z, carry_scr[...], tri, None)
            carry_scr[...] = carry
            accumulate(w, lambda h: v_ref[pl.ds(h, PAGE_SIZE, stride=N_HEADS), :].astype(BF16))

    @pl.when(s == pl.num_programs(1) - 1)
    def _():
        o_ref[...] = acc_scr[...].astype(o_ref.dtype)


def sb_sample_attention(q2d, kn2d, vn2d, bias, cache_k, cache_v, page_table, layer, bsz, t_new,
                        pages_per_step):
    n_pages = page_table.shape[1]
    n_steps = n_pages // pages_per_step
    ck = cache_k.reshape(cache_k.shape[0], cache_k.shape[1], PAGE_SIZE * N_HEADS, HEAD_DIM)
    cv = cache_v.reshape(cache_v.shape[0], cache_v.shape[1], PAGE_SIZE * N_HEADS, HEAD_DIM)
    brow = jnp.repeat(bias, t_new)[:, None]
    pad_new = lambda a: jnp.pad(a.reshape(bsz, t_new, D_MODEL), ((0, 0), (0, PAGE_SIZE - t_new), (0, 0)))

    def page_spec(p):
        def index_map(b, s, pt):
            page = n_pages - 1 - (jnp.maximum(s, 1) - 1) * pages_per_step - p
            return (layer, pt[b * n_pages + page], 0, 0)
        return pl.BlockSpec((None, None, PAGE_SIZE * N_HEADS, HEAD_DIM), index_map)

    row_spec = pl.BlockSpec((t_new, D_MODEL), lambda b, s, pt: (b, 0))
    new_spec = pl.BlockSpec((None, PAGE_SIZE, D_MODEL), lambda b, s, pt: (b, 0, 0))
    in_specs = [row_spec, new_spec, new_spec,
                pl.BlockSpec((N_HEADS * t_new, 1), lambda b, s, pt: (0, 0))]
    in_specs += [page_spec(p) for p in range(pages_per_step)] * 2
    return pl.pallas_call(
        functools.partial(_sb_sample_kernel, pages_per_step=pages_per_step, t_new=t_new),
        grid_spec=pltpu.PrefetchScalarGridSpec(
            num_scalar_prefetch=1,
            grid=(bsz, n_steps + 1),
            in_specs=in_specs,
            out_specs=pl.BlockSpec((t_new, D_MODEL), lambda b, s, pt: (b, 0)),
            scratch_shapes=[pltpu.VMEM((N_HEADS * t_new, 1), F32), pltpu.VMEM((t_new, D_MODEL), F32)]),
        out_shape=jax.ShapeDtypeStruct((bsz * t_new, D_MODEL), F32),
        compiler_params=_params(("parallel", "arbitrary")),
    )(page_table.reshape(-1), q2d, pad_new(kn2d), pad_new(vn2d), brow,
      *([ck] * pages_per_step), *([cv] * pages_per_step))


def _convpool_kernel(glu_ref, bin_ref, ha_ref, hb_ref, x_ref, cw_ref, cb_ref, lg_ref, lb_ref,
                     pw_ref, ps_ref, wo_ref, o_ref, ea_scr, eb_scr, *, nb, tt, start_pos):
    i = pl.program_id(1)
    ea_scr[:, :CONV_HALO, :] = ha_ref[:, 0]
    ea_scr[:, CONV_HALO:, :] = glu_ref[...]
    eb_scr[:, :POOL_HALO, :] = hb_ref[:, 0]
    eb_scr[:, POOL_HALO:, :] = bin_ref[...]

    first = CONV_HALO - (CONV_W - 1)
    conv = jnp.zeros((nb, tt, D_A), F32) + cb_ref[...]
    for k in range(CONV_W):
        conv = conv + ea_scr[:, first + k:first + k + tt, :] * cw_ref[k:k + 1, :]
    mu = jnp.mean(conv, axis=-1, keepdims=True)
    cen = conv - mu
    var = jnp.mean(cen * cen, axis=-1, keepdims=True)
    ln = cen * lax.rsqrt(var + LN_EPS) * lg_ref[...] + lb_ref[...]
    a = ln * jax.nn.sigmoid(ln)

    pos = start_pos + i * tt + lax.broadcasted_iota(jnp.int32, (nb, tt, POOL_GROUP), 1)
    ys = []
    for g, wlen in enumerate(POOL_WINDOWS):
        lanes = slice(g * POOL_GROUP, (g + 1) * POOL_GROUP)
        tok = eb_scr[:, POOL_HALO:, lanes]
        tot = tok
        for j in range(1, wlen):
            tot = tot + eb_scr[:, POOL_HALO - j:POOL_HALO - j + tt, lanes]
        cnt = jnp.minimum(wlen, pos + 1).astype(F32)
        d = (tot / cnt - tok).reshape(nb * tt, POOL_GROUP).astype(BF16)
        ys.append(jnp.dot(d, pw_ref[g], preferred_element_type=F32))
    b = jnp.concatenate(ys, axis=-1) * ps_ref[...]

    mixed = jnp.concatenate([a.reshape(nb * tt, D_A), b], axis=-1).astype(BF16)
    out = jnp.dot(mixed, wo_ref[...], preferred_element_type=F32)
    o_ref[...] = x_ref[...] + out.reshape(nb, tt, D_MODEL)


def convpool_mix(glu, b_in, halo_a, halo_b, x, cw, cb, lg, lb, pw_all, ps, wo_all, layer, nb, tt,
                 start_pos):
    bsz, t, _ = x.shape
    row = lambda c: pl.BlockSpec((nb, tt, c), lambda b, i: (b, i, 0))
    vec = lambda r, c: pl.BlockSpec((r, c), lambda b, i: (0, 0))
    return pl.pallas_call(
        functools.partial(_convpool_kernel, nb=nb, tt=tt, start_pos=start_pos),
        grid=(bsz // nb, t // tt),
        in_specs=[row(D_A), row(D_B),
                  pl.BlockSpec((nb, 1, CONV_HALO, D_A), lambda b, i: (b, i, 0, 0)),
                  pl.BlockSpec((nb, 1, POOL_HALO, D_B), lambda b, i: (b, i, 0, 0)),
                  row(D_MODEL), vec(CONV_W, D_A), vec(1, D_A), vec(1, D_A), vec(1, D_A),
                  pl.BlockSpec((None, len(POOL_WINDOWS), POOL_GROUP, POOL_GROUP),
                               lambda b, i: (layer, 0, 0, 0)),
                  vec(1, D_B),
                  pl.BlockSpec((None, D_A + D_B, D_MODEL), lambda b, i: (layer, 0, 0))],
        out_specs=row(D_MODEL),
        out_shape=jax.ShapeDtypeStruct((bsz, t, D_MODEL), F32),
        scratch_shapes=[pltpu.VMEM((nb, CONV_HALO + tt, D_A), F32),
                        pltpu.VMEM((nb, POOL_HALO + tt, D_B), F32)],
        compiler_params=_params(("parallel", "arbitrary")),
    )(glu, b_in, halo_a, halo_b, x, cw, cb, lg, lb, pw_all, ps, wo_all)


def _tile_halos(stream, state, tt, halo):
    bsz, t, c = stream.shape
    first = jnp.pad(state, ((0, 0), (halo - state.shape[1], 0), (0, 0)))[:, None]
    if t == tt:
        return first
    rest = stream.reshape(bsz, t // tt, tt, c)[:, :-1, tt - halo:, :]
    return jnp.concatenate([first, rest], axis=1)


def _ffn_kernel(x_ref, xh_ref, st_ref, g_ref, wa_ref, wg_ref, cw_ref, cb_ref, wd_ref, o_ref, at_ref,
                h_scr, a_scr, *, nb, tt, hl, recompute):
    i = pl.program_id(1)
    f = pl.program_id(2)
    hx = hl if recompute else 0
    tf = wa_ref.shape[-1]

    @pl.when(f == 0)
    def _():
        x = x_ref[...]
        h_scr[hx:, :] = _rms(x, g_ref[...]).reshape(nb * tt, D_MODEL).astype(BF16)
        if recompute:
            h_scr[:hx, :] = _rms(xh_ref[0, 0], g_ref[...]).astype(BF16)
        o_ref[...] = x

    h = h_scr[...]
    a_all = jnp.dot(h, wa_ref[...], preferred_element_type=F32).reshape(nb, hx + tt, tf)
    gate = jnp.dot(h, wg_ref[...], preferred_element_type=F32).reshape(nb, hx + tt, tf)[:, hx:, :]
    if recompute:
        a_scr[...] = a_all

        @pl.when(i == 0)
        def _():
            a_scr[:, :hl, :] = st_ref[...]
    else:
        a_scr[:, :hl, :] = st_ref[...]
        a_scr[:, hl:, :] = a_all

    conv = cb_ref[...] + a_scr[:, hl:, :] * cw_ref[2:3, :]
    conv = conv + a_scr[:, hl - 1:hl - 1 + tt, :] * cw_ref[1:2, :]
    conv = conv + a_scr[:, hl - 2:hl - 2 + tt, :] * cw_ref[0:1, :]
    y = (jax.nn.gelu(conv) * gate).reshape(nb * tt, tf).astype(BF16)
    o_ref[...] += jnp.dot(y, wd_ref[...], preferred_element_type=F32).reshape(nb, tt, D_MODEL)
    at_ref[:, 0] = a_scr[:, hl + tt - 8:, :]


def conv_ffn(x, x_halo, state, g, w_up_all, cw, cb, w_down_all, layer, nb, tt, tf, hl, recompute):
    bsz, t, _ = x.shape
    nt = t // tt
    nf = D_FF // tf
    hx = hl if recompute else 0
    return pl.pallas_call(
        functools.partial(_ffn_kernel, nb=nb, tt=tt, hl=hl, recompute=recompute),
        grid=(bsz // nb, nt, nf),
        in_specs=[pl.BlockSpec((nb, tt, D_MODEL), lambda b, i, f: (b, i, 0)),
                  pl.BlockSpec((nb, 1, x_halo.shape[2], D_MODEL), lambda b, i, f: (b, i, 0, 0)),
                  pl.BlockSpec((nb, hl, tf), lambda b, i, f: (b, 0, f)),
                  pl.BlockSpec((1, D_MODEL), lambda b, i, f: (0, 0)),
                  pl.BlockSpec((None, D_MODEL, tf), lambda b, i, f: (layer, 0, f)),
                  pl.BlockSpec((None, D_MODEL, tf), lambda b, i, f: (layer, 0, f + nf)),
                  pl.BlockSpec((FFN_CONV_W, tf), lambda b, i, f: (0, f)),
                  pl.BlockSpec((1, tf), lambda b, i, f: (0, f)),
                  pl.BlockSpec((None, tf, D_MODEL), lambda b, i, f: (layer, f, 0))],
        out_specs=[pl.BlockSpec((nb, tt, D_MODEL), lambda b, i, f: (b, i, 0)),
                   pl.BlockSpec((nb, 1, 8, tf), lambda b, i, f: (b, i, 0, f))],
        out_shape=[jax.ShapeDtypeStruct((bsz, t, D_MODEL), F32),
                   jax.ShapeDtypeStruct((bsz, nt, 8, D_FF), F32)],
        scratch_shapes=[pltpu.VMEM((nb * (hx + tt), D_MODEL), BF16),
                        pltpu.VMEM((nb, hl + tt, tf), F32)],
        compiler_params=_params(("parallel", "parallel", "arbitrary")),
    )(x, x_halo, state, g, w_up_all, w_up_all, cw, cb, w_down_all)


def _final_norm_kernel(x_ref, g_ref, o_ref):
    o_ref[...] = _rms(x_ref[...], g_ref[...])


def final_norm(x2d, g, tm):
    m = x2d.shape[0]
    return pl.pallas_call(
        _final_norm_kernel,
        grid=(m // tm,),
        in_specs=[pl.BlockSpec((tm, D_MODEL), lambda i: (i, 0)),
                  pl.BlockSpec((1, D_MODEL), lambda i: (0, 0))],
        out_specs=pl.BlockSpec((tm, D_MODEL), lambda i: (i, 0)),
        out_shape=jax.ShapeDtypeStruct((m, D_MODEL), F32),
        compiler_params=_params(("parallel",)),
    )(x2d, g)


def _trunk(x, start_pos, conv_prev, pool_prev, ffn_prev, cache_k, cache_v, page_table, p, cfg):
    bsz, t, _ = x.shape
    m = bsz * t
    nb, tt = cfg["nb"], cfg["tt"]
    convs, pools, ks, vs, ffns = [], [], [], [], []
    for layer in range(DEPTH):
        li = layer // 2
        g_mix = p["norm_mix"][layer][None, :]
        x2d = x.reshape(m, D_MODEL)
        if layer % 2 == 0:
            glu, b_in = norm_proj(x2d, g_mix, p["w_in_cb"], li, 3, D_A, (F32, F32), _glu_epilogue,
                                  cfg["tm"], cfg["tn"])
            glu = glu.reshape(bsz, t, D_A)
            b_in = b_in.reshape(bsz, t, D_B)
            x = convpool_mix(glu, b_in,
                             _tile_halos(glu, conv_prev[li], tt, CONV_HALO),
                             _tile_halos(b_in, pool_prev[li], tt, POOL_HALO),
                             x, p["conv_w"][li], p["conv_b"][li][None, :], p["ln_a_g"][li][None, :],
                             p["ln_a_b"][li][None, :], p["pool_w"], p["pool_scale"][li][None, :],
                             p["w_out_cb"], li, nb, tt, start_pos)
            convs.append(jnp.concatenate([conv_prev[li], glu], axis=1)[:, -(CONV_W - 1):])
            pools.append(jnp.concatenate([pool_prev[li], b_in], axis=1)[:, -(MAX_WIN - 1):])
        else:
            q_dtype = BF16 if page_table is None else F32
            q, k, v = norm_proj(x2d, g_mix, p["w_qkv"], li, 3, D_MODEL, (q_dtype, F32, F32),
                                _qkv_epilogue, cfg["tm"], cfg["tn"])
            if page_table is None:
                o = sb_prompt_attention(q, k, v, p["sb_bias"][li], bsz, t, cfg["tq"])
            else:
                o = sb_sample_attention(q, k, v, p["sb_bias"][li], cache_k, cache_v, page_table, li,
                                        bsz, t, cfg["pages_per_step"])
            x = proj_res(o, p["w_o_sb"], li, x2d, cfg["tm"], cfg["tn"]).reshape(bsz, t, D_MODEL)
            ks.append(k.reshape(bsz, t, N_HEADS, HEAD_DIM))
            vs.append(v.reshape(bsz, t, N_HEADS, HEAD_DIM))
        hl = cfg["ffn_halo"]
        if cfg["ffn_recompute"]:
            x_halo = _tile_halos(x, jnp.zeros((bsz, 0, D_MODEL), F32), cfg["ffn_tt"], hl)
        else:
            x_halo = jnp.zeros((bsz, 1, 8, D_MODEL), F32)
        state = jnp.pad(ffn_prev[layer], ((0, 0), (hl - (FFN_CONV_W - 1), 0), (0, 0)))
        x, a_tail = conv_ffn(x, x_halo, state, p["norm_ffn"][layer][None, :], p["w_up"],
                             p["ffn_conv_w"][layer], p["ffn_conv_b"][layer][None, :], p["w_down"],
                             layer, cfg["ffn_nb"], cfg["ffn_tt"], cfg["tf"], hl, cfg["ffn_recompute"])
        ffns.append(a_tail[:, -1, -(FFN_CONV_W - 1):, :])
    y = final_norm(x.reshape(m, D_MODEL), p["norm_final"][None, :], cfg["tm"]).reshape(bsz, t, D_MODEL)
    return y, jnp.stack(convs), jnp.stack(pools), jnp.stack(ks), jnp.stack(vs), jnp.stack(ffns)


PROMPT_CFG = dict(nb=1, tt=256, tm=1024, tn=512, tq=512, ffn_nb=1, ffn_tt=512, tf=512, ffn_halo=16,
                  ffn_recompute=True)
SAMPLE_CFG = dict(nb=8, tt=8, tm=64, tn=512, pages_per_step=4, ffn_nb=8, ffn_tt=8, tf=512, ffn_halo=8,
                  ffn_recompute=False)


def kernel(x_prompt, x_sample, state_conv, state_pool, cache_k, cache_v, page_table, state_ffn, norm_mix, norm_ffn, norm_final, w_in_cb, conv_w, conv_b, ln_a_g, ln_a_b, pool_w, pool_scale, w_out_cb, w_qkv, w_o_sb, sb_bias, w_up, ffn_conv_w, ffn_conv_b, w_down):
    bp = x_prompt.shape[0]
    p = dict(norm_mix=norm_mix, norm_ffn=norm_ffn, norm_final=norm_final,
             w_in_cb=w_in_cb.astype(BF16), conv_w=conv_w, conv_b=conv_b, ln_a_g=ln_a_g, ln_a_b=ln_a_b,
             pool_w=pool_w.astype(BF16), pool_scale=pool_scale, w_out_cb=w_out_cb.astype(BF16),
             w_qkv=w_qkv.astype(BF16), w_o_sb=w_o_sb.astype(BF16), sb_bias=sb_bias,
             w_up=w_up.astype(BF16), ffn_conv_w=ffn_conv_w, ffn_conv_b=ffn_conv_b,
             w_down=w_down.astype(BF16))
    n_cb, n_ffn = state_conv.shape[0], state_ffn.shape[0]
    zero_conv = jnp.zeros((n_cb, bp, CONV_W - 1, D_A), F32)
    zero_pool = jnp.zeros((n_cb, bp, MAX_WIN - 1, D_B), F32)
    zero_ffn = jnp.zeros((n_ffn, bp, FFN_CONV_W - 1, D_FF), F32)
    out_p = _trunk(x_prompt, 0, zero_conv, zero_pool, zero_ffn, None, None, None, p, PROMPT_CFG)
    past_len = page_table.shape[1] * PAGE_SIZE
    out_s = _trunk(x_sample, past_len, state_conv, state_pool, state_ffn, cache_k, cache_v,
                   page_table, p, SAMPLE_CFG)
    return (out_p[0], out_s[0]) + out_p[1:] + out_s[1:]
```

```python
import functools

import jax
import jax.numpy as jnp
from jax import lax
from jax.experimental import pallas as pl
from jax.experimental.pallas import tpu as pltpu

F32 = jnp.float32
BF16 = jnp.bfloat16

D_MODEL = 2048
DEPTH = 4
PAGE_SIZE = 128
D_A = D_MODEL // 2
D_B = D_MODEL // 2
CONV_W = 31
POOL_WINDOWS = (2, 4, 8, 16)
POOL_GROUP = D_B // len(POOL_WINDOWS)
MAX_WIN = 16
N_HEADS = 16
HEAD_DIM = D_MODEL // N_HEADS
D_FF = 5632
FFN_CONV_W = 3
RMS_EPS = 1e-6
LN_EPS = 1e-5
LOG2E = 1.4426950408889634

V7X_VMEM_LIMIT_BYTES = 56 * 1024 * 1024
SUBLANES = 8
CONV_HALO = 32
POOL_HALO = 16
KEY_SUB = 256
HEADS_PER_STEP = 2


def _params(sem):
    return pltpu.CompilerParams(dimension_semantics=sem, vmem_limit_bytes=V7X_VMEM_LIMIT_BYTES)


def _rms(x, g):
    ms = jnp.mean(x * x, axis=-1, keepdims=True)
    return x * lax.rsqrt(ms + RMS_EPS) * g


def _norm_proj_kernel(x_ref, g_ref, *refs, n_w, epilogue):
    w_refs = refs[:n_w]
    o_refs = refs[n_w:-1]
    h_scr = refs[-1]

    @pl.when(pl.program_id(1) == 0)
    def _():
        h_scr[...] = _rms(x_ref[...], g_ref[...]).astype(BF16)

    h = h_scr[...]
    outs = epilogue(*[jnp.dot(h, w[...], preferred_element_type=F32) for w in w_refs])
    for o_ref, v in zip(o_refs, outs):
        o_ref[...] = v.astype(o_ref.dtype)


def _glu_epilogue(a_val, a_gate, b_in):
    return a_val * jax.nn.sigmoid(a_gate), b_in


def _qkv_epilogue(q, k, v):
    return q * (HEAD_DIM ** -0.5 * LOG2E), k, v


def norm_proj(x2d, g, w_all, layer, n_w, n_each, out_dtypes, epilogue, tm, tn):
    m = x2d.shape[0]
    nj = n_each // tn
    in_specs = [pl.BlockSpec((tm, D_MODEL), lambda i, j: (i, 0)),
                pl.BlockSpec((1, D_MODEL), lambda i, j: (0, 0))]
    for k in range(n_w):
        in_specs.append(pl.BlockSpec((None, D_MODEL, tn), lambda i, j, k=k: (layer, 0, j + k * nj)))
    return pl.pallas_call(
        functools.partial(_norm_proj_kernel, n_w=n_w, epilogue=epilogue),
        grid=(m // tm, nj),
        in_specs=in_specs,
        out_specs=[pl.BlockSpec((tm, tn), lambda i, j: (i, j)) for _ in out_dtypes],
        out_shape=[jax.ShapeDtypeStruct((m, n_each), dt) for dt in out_dtypes],
        scratch_shapes=[pltpu.VMEM((tm, D_MODEL), BF16)],
        compiler_params=_params(("parallel", "arbitrary")),
    )(x2d, g, *([w_all] * n_w))


def _proj_res_kernel(a_ref, w_ref, r_ref, o_ref):
    o_ref[...] = r_ref[...] + jnp.dot(a_ref[...].astype(BF16), w_ref[...], preferred_element_type=F32)


def proj_res(a2d, w_all, layer, res2d, tm, tn):
    m, kdim = a2d.shape
    n = res2d.shape[1]
    return pl.pallas_call(
        _proj_res_kernel,
        grid=(m // tm, n // tn),
        in_specs=[pl.BlockSpec((tm, kdim), lambda i, j: (i, 0)),
                  pl.BlockSpec((None, kdim, tn), lambda i, j: (layer, 0, j)),
                  pl.BlockSpec((tm, tn), lambda i, j: (i, j))],
        out_specs=pl.BlockSpec((tm, tn), lambda i, j: (i, j)),
        out_shape=jax.ShapeDtypeStruct((m, n), F32),
        compiler_params=_params(("parallel", "arbitrary")),
    )(a2d, w_all, res2d)


def _strict_lower_ones(n):
    return jnp.tril(jnp.ones((n, n), BF16), -1)


def _sb_log_terms(z2, mask):
    nz = -z2
    soft = jnp.log2(1.0 + jnp.exp2(jnp.minimum(z2, nz)))
    ls = jnp.minimum(nz, 0.0) - soft
    lsig = z2 + ls
    if mask is not None:
        ls = jnp.where(mask, ls, 0.0)
    return ls, lsig


def _sb_suffix(ls, tri):
    return jnp.dot(ls.astype(BF16), tri, preferred_element_type=F32)


def _sb_finish(lsig, rest, carry, mask):
    w = jnp.exp2(lsig + rest + carry)
    if mask is not None:
        w = jnp.where(mask, w, 0.0)
    return w


def _sb_prompt_kernel(bias_ref, tri_ref, q_ref, k_ref, v_ref, o_ref, kb_scr, vb_scr, carry_scr, acc_scr,
                      *, tq):
    hg = pl.program_id(1)
    qi = pl.program_id(2)

    @pl.when(qi == 0)
    def _():
        kb_scr[...] = k_ref[...].astype(BF16)
        vb_scr[...] = v_ref[...].astype(BF16)

    tri = tri_ref[...]
    carry_scr[...] = jnp.zeros_like(carry_scr)
    acc_scr[...] = jnp.zeros_like(acc_scr)
    n_diag = tq // KEY_SUB
    heads = range(HEADS_PER_STEP)
    lanes = [slice(j * HEAD_DIM, (j + 1) * HEAD_DIM) for j in heads]

    def block(kb, masked):
        start = pl.multiple_of(kb * KEY_SUB, KEY_SUB)
        mask = None
        if masked:
            q_pos = qi * tq + lax.broadcasted_iota(jnp.int32, (tq, KEY_SUB), 0)
            k_pos = kb * KEY_SUB + lax.broadcasted_iota(jnp.int32, (tq, KEY_SUB), 1)
            mask = k_pos < q_pos
        zs = [lax.dot_general(q_ref[:, lanes[j]], kb_scr[pl.ds(start, KEY_SUB), lanes[j]],
                              (((1,), (1,)), ((), ())), preferred_element_type=F32)
              + bias_ref[hg * HEADS_PER_STEP + j] for j in heads]
        terms = [_sb_log_terms(z, mask) for z in zs]
        rests = [_sb_suffix(ls, tri) for ls, _ in terms]
        for j in heads:
            ls, lsig = terms[j]
            w = _sb_finish(lsig, rests[j], carry_scr[j], mask)
            carry_scr[j] += jnp.sum(ls, axis=-1, keepdims=True)
            acc_scr[:, lanes[j]] += jnp.dot(w.astype(BF16), vb_scr[pl.ds(start, KEY_SUB), lanes[j]],
                                            preferred_element_type=F32)

    for d in range(n_diag):
        block(qi * n_diag + (n_diag - 1 - d), True)

    def body(j, c):
        block(qi * n_diag - 1 - j, False)
        return c

    lax.fori_loop(0, qi * n_diag, body, 0)
    o_ref[...] = acc_scr[...].astype(o_ref.dtype)


def sb_prompt_attention(q2d, k2d, v2d, bias2, bsz, t, tq):
    nq = t // tq
    gw = HEADS_PER_STEP * HEAD_DIM
    return pl.pallas_call(
        functools.partial(_sb_prompt_kernel, tq=tq),
        grid=(bsz, N_HEADS // HEADS_PER_STEP, nq),
        in_specs=[pl.BlockSpec(memory_space=pltpu.SMEM),
                  pl.BlockSpec((KEY_SUB, KEY_SUB), lambda b, h, i: (0, 0)),
                  pl.BlockSpec((tq, gw), lambda b, h, i: (b * nq + i, h)),
                  pl.BlockSpec((t, gw), lambda b, h, i: (b, h)),
                  pl.BlockSpec((t, gw), lambda b, h, i: (b, h))],
        out_specs=pl.BlockSpec((tq, gw), lambda b, h, i: (b * nq + i, h)),
        out_shape=jax.ShapeDtypeStruct((bsz * t, D_MODEL), BF16),
        scratch_shapes=[pltpu.VMEM((t, gw), BF16), pltpu.VMEM((t, gw), BF16),
                        pltpu.VMEM((HEADS_PER_STEP, tq, 1), F32), pltpu.VMEM((tq, gw), F32)],
        compiler_params=_params(("parallel", "parallel", "arbitrary")),
    )(bias2, _strict_lower_ones(KEY_SUB), q2d, k2d, v2d)


def _sb_sample_kernel(pt_ref, q_ref, kn_ref, vn_ref, brow_ref, tri_ref, *refs, pages_per_step, t_new):
    k_refs = refs[:pages_per_step]
    v_refs = refs[pages_per_step:2 * pages_per_step]
    o_ref = refs[2 * pages_per_step]
    carry_scr, acc_scr = refs[2 * pages_per_step + 1:]
    s = pl.program_id(1)
    rows = N_HEADS * t_new
    brow = brow_ref[...]
    head_lanes = [slice(h * HEAD_DIM, (h + 1) * HEAD_DIM) for h in range(N_HEADS)]

    def attend(key_of_head, value_of_head, tri, mask, carry):
        parts = [lax.dot_general(q_ref[:, head_lanes[h]].astype(BF16), key_of_head(h),
                                 (((1,), (1,)), ((), ())), preferred_element_type=F32)
                 for h in range(N_HEADS)]
        z = jnp.concatenate(parts, axis=0) + brow
        ls, lsig = _sb_log_terms(z, mask)
        w = _sb_finish(lsig, _sb_suffix(ls, tri), carry, mask)
        for h in range(N_HEADS):
            wh = w[h * t_new:(h + 1) * t_new, :].astype(BF16)
            acc_scr[:, head_lanes[h]] += jnp.dot(wh, value_of_head(h), preferred_element_type=F32)
        carry_scr[...] = carry + jnp.sum(ls, axis=-1, keepdims=True)

    @pl.when(s == 0)
    def _():
        acc_scr[...] = jnp.zeros_like(acc_scr)
        r = lax.broadcasted_iota(jnp.int32, (rows, PAGE_SIZE), 0)
        c = lax.broadcasted_iota(jnp.int32, (rows, PAGE_SIZE), 1)
        attend(lambda h: kn_ref[:, head_lanes[h]].astype(BF16),
               lambda h: vn_ref[:, head_lanes[h]].astype(BF16),
               tri_ref[...], c < (r % t_new), jnp.zeros((rows, 1), F32))

    @pl.when(s > 0)
    def _():
        for k_ref, v_ref in reversed(list(zip(k_refs, v_refs))):
            attend(lambda h: k_ref[pl.ds(h, PAGE_SIZE, stride=N_HEADS), :].astype(BF16),
                   lambda h: v_ref[pl.ds(h, PAGE_SIZE, stride=N_HEADS), :].astype(BF16),
                   tri_ref[...], None, carry_scr[...])

    @pl.when(s == pl.num_programs(1) - 1)
    def _():
        o_ref[...] = acc_scr[...].astype(o_ref.dtype)


def sb_sample_attention(q2d, kn2d, vn2d, bias2, cache_k, cache_v, page_table, layer, bsz, t_new,
                        pages_per_step):
    n_pages = page_table.shape[1]
    n_steps = n_pages // pages_per_step
    ck = cache_k.reshape(cache_k.shape[0], cache_k.shape[1], PAGE_SIZE * N_HEADS, HEAD_DIM)
    cv = cache_v.reshape(cache_v.shape[0], cache_v.shape[1], PAGE_SIZE * N_HEADS, HEAD_DIM)
    brow = jnp.repeat(bias2, t_new)[:, None]
    pad_new = lambda a: jnp.pad(a.reshape(bsz, t_new, D_MODEL), ((0, 0), (0, PAGE_SIZE - t_new), (0, 0)))

    def page_spec(p):
        def index_map(b, s, pt):
            page = n_pages - jnp.maximum(s, 1) * pages_per_step + p
            return (layer, pt[b * n_pages + page], 0, 0)
        return pl.BlockSpec((None, None, PAGE_SIZE * N_HEADS, HEAD_DIM), index_map)

    row_spec = pl.BlockSpec((t_new, D_MODEL), lambda b, s, pt: (b, 0))
    new_spec = pl.BlockSpec((None, PAGE_SIZE, D_MODEL), lambda b, s, pt: (b, 0, 0))
    in_specs = [row_spec, new_spec, new_spec,
                pl.BlockSpec((N_HEADS * t_new, 1), lambda b, s, pt: (0, 0)),
                pl.BlockSpec((PAGE_SIZE, PAGE_SIZE), lambda b, s, pt: (0, 0))]
    in_specs += [page_spec(p) for p in range(pages_per_step)] * 2
    return pl.pallas_call(
        functools.partial(_sb_sample_kernel, pages_per_step=pages_per_step, t_new=t_new),
        grid_spec=pltpu.PrefetchScalarGridSpec(
            num_scalar_prefetch=1,
            grid=(bsz, n_steps + 1),
            in_specs=in_specs,
            out_specs=pl.BlockSpec((t_new, D_MODEL), lambda b, s, pt: (b, 0)),
            scratch_shapes=[pltpu.VMEM((N_HEADS * t_new, 1), F32), pltpu.VMEM((t_new, D_MODEL), F32)]),
        out_shape=jax.ShapeDtypeStruct((bsz * t_new, D_MODEL), F32),
        compiler_params=_params(("parallel", "arbitrary")),
    )(page_table.reshape(-1), q2d, pad_new(kn2d), pad_new(vn2d), brow, _strict_lower_ones(PAGE_SIZE),
      *([ck] * pages_per_step), *([cv] * pages_per_step))


def _convpool_kernel(glu_ref, bin_ref, ha_ref, hb_ref, x_ref, cw_ref, cb_ref, lg_ref, lb_ref,
                     pw_ref, ps_ref, wo_ref, o_ref, ea_scr, eb_scr, sh_scr, *, nb, tt, start_pos):
    i = pl.program_id(1)
    ea_scr[:, :CONV_HALO, :] = ha_ref[:, 0]
    ea_scr[:, CONV_HALO:, :] = glu_ref[...]
    eb_scr[:, :POOL_HALO, :] = hb_ref[:, 0]
    eb_scr[:, POOL_HALO:, :] = bin_ref[...]

    first = CONV_HALO - (CONV_W - 1)
    conv = jnp.zeros((nb, tt, D_A), F32) + cb_ref[...]
    for r in range(SUBLANES):
        taps = [k for k in range(CONV_W) if (first + k) % SUBLANES == r]
        span = max(first + k for k in taps) - r + tt
        if r:
            sh_scr[r - 1, :, :span, :] = ea_scr[:, r:r + span, :]
        for k in taps:
            off = first + k - r
            rows = sh_scr[r - 1, :, off:off + tt, :] if r else ea_scr[:, off:off + tt, :]
            conv = conv + rows * cw_ref[k:k + 1, :]
    mu = jnp.mean(conv, axis=-1, keepdims=True)
    cen = conv - mu
    var = jnp.mean(cen * cen, axis=-1, keepdims=True)
    ln = cen * lax.rsqrt(var + LN_EPS) * lg_ref[...] + lb_ref[...]
    a = ln * jax.nn.sigmoid(ln)

    pos = start_pos + i * tt + lax.broadcasted_iota(jnp.int32, (nb, tt, POOL_GROUP), 1)
    ys = []
    for g, wlen in enumerate(POOL_WINDOWS):
        lanes = slice(g * POOL_GROUP, (g + 1) * POOL_GROUP)
        tok = eb_scr[:, POOL_HALO:, lanes]
        tot = tok
        for j in range(1, wlen):
            tot = tot + eb_scr[:, POOL_HALO - j:POOL_HALO - j + tt, lanes]
        cnt = jnp.minimum(wlen, pos + 1).astype(F32)
        d = (tot / cnt - tok).reshape(nb * tt, POOL_GROUP).astype(BF16)
        ys.append(jnp.dot(d, pw_ref[g], preferred_element_type=F32))
    b = jnp.concatenate(ys, axis=-1) * ps_ref[...]

    mixed = jnp.concatenate([a.reshape(nb * tt, D_A), b], axis=-1).astype(BF16)
    out = jnp.dot(mixed, wo_ref[...], preferred_element_type=F32)
    o_ref[...] = x_ref[...] + out.reshape(nb, tt, D_MODEL)


def convpool_mix(glu, b_in, halo_a, halo_b, x, cw, cb, lg, lb, pw_all, ps, wo_all, layer, nb, tt,
                 start_pos):
    bsz, t, _ = x.shape
    row = lambda c: pl.BlockSpec((nb, tt, c), lambda b, i: (b, i, 0))
    vec = lambda r, c: pl.BlockSpec((r, c), lambda b, i: (0, 0))
    return pl.pallas_call(
        functools.partial(_convpool_kernel, nb=nb, tt=tt, start_pos=start_pos),
        grid=(bsz // nb, t // tt),
        in_specs=[row(D_A), row(D_B),
                  pl.BlockSpec((nb, 1, CONV_HALO, D_A), lambda b, i: (b, i, 0, 0)),
                  pl.BlockSpec((nb, 1, POOL_HALO, D_B), lambda b, i: (b, i, 0, 0)),
                  row(D_MODEL), vec(CONV_W, D_A), vec(1, D_A), vec(1, D_A), vec(1, D_A),
                  pl.BlockSpec((None, len(POOL_WINDOWS), POOL_GROUP, POOL_GROUP),
                               lambda b, i: (layer, 0, 0, 0)),
                  vec(1, D_B),
                  pl.BlockSpec((None, D_A + D_B, D_MODEL), lambda b, i: (layer, 0, 0))],
        out_specs=row(D_MODEL),
        out_shape=jax.ShapeDtypeStruct((bsz, t, D_MODEL), F32),
        scratch_shapes=[pltpu.VMEM((nb, CONV_HALO + tt, D_A), F32),
                        pltpu.VMEM((nb, POOL_HALO + tt, D_B), F32),
                        pltpu.VMEM((SUBLANES - 1, nb, CONV_HALO - SUBLANES + tt, D_A), F32)],
        compiler_params=_params(("parallel", "arbitrary")),
    )(glu, b_in, halo_a, halo_b, x, cw, cb, lg, lb, pw_all, ps, wo_all)


def _tile_halos(stream, state, tt, halo):
    bsz, t, c = stream.shape
    first = jnp.pad(state, ((0, 0), (halo - state.shape[1], 0), (0, 0)))[:, None]
    if t == tt:
        return first
    rest = stream.reshape(bsz, t // tt, tt, c)[:, :-1, tt - halo:, :]
    return jnp.concatenate([first, rest], axis=1)


def _ffn_kernel(x_ref, xh_ref, st_ref, g_ref, wa_ref, wg_ref, cw_ref, cb_ref, wdp_ref, wdl_ref,
                o_ref, at_ref, h_scr, a_scr, y_scr, *, nb, tt, hl, recompute):
    i = pl.program_id(1)
    f = pl.program_id(2)
    nf = pl.num_programs(2)
    hx = hl if recompute else 0
    tf = wa_ref.shape[-1]

    @pl.when(f == 0)
    def _():
        x = x_ref[...]
        h_scr[hx:, :] = _rms(x, g_ref[...]).reshape(nb * tt, D_MODEL).astype(BF16)
        if recompute:
            h_scr[:hx, :] = _rms(xh_ref[0, 0], g_ref[...]).astype(BF16)
        o_ref[...] = x
        y_scr[1] = jnp.zeros(y_scr.shape[1:], BF16)

    h = h_scr[...]
    a_all = jnp.dot(h, wa_ref[...], preferred_element_type=F32).reshape(nb, hx + tt, tf)
    gate = jnp.dot(h, wg_ref[...], preferred_element_type=F32).reshape(nb, hx + tt, tf)[:, hx:, :]
    o_ref[...] += jnp.dot(y_scr[(f + 1) % 2], wdp_ref[...],
                          preferred_element_type=F32).reshape(nb, tt, D_MODEL)
    if recompute:
        a_scr[:, :hl, :] = jnp.where(i == 0, st_ref[...], a_all[:, :hl, :])
        a_scr[:, hl:, :] = a_all[:, hl:, :]
    else:
        a_scr[:, :hl, :] = st_ref[...]
        a_scr[:, hl:, :] = a_all

    conv = cb_ref[...] + a_scr[:, hl:, :] * cw_ref[2:3, :]
    conv = conv + a_scr[:, hl - 1:hl - 1 + tt, :] * cw_ref[1:2, :]
    conv = conv + a_scr[:, hl - 2:hl - 2 + tt, :] * cw_ref[0:1, :]
    y_scr[f % 2] = (jax.nn.gelu(conv) * gate).reshape(nb * tt, tf).astype(BF16)
    at_ref[:, 0] = a_scr[:, hl + tt - SUBLANES:, :]

    @pl.when(f == nf - 1)
    def _():
        o_ref[...] += jnp.dot(y_scr[f % 2], wdl_ref[...],
                              preferred_element_type=F32).reshape(nb, tt, D_MODEL)


def conv_ffn(x, x_halo, state, g, w_up_all, cw, cb, w_down_all, layer, nb, tt, tf, hl, recompute):
    bsz, t, _ = x.shape
    nt = t // tt
    nf = D_FF // tf
    hx = hl if recompute else 0
    return pl.pallas_call(
        functools.partial(_ffn_kernel, nb=nb, tt=tt, hl=hl, recompute=recompute),
        grid=(bsz // nb, nt, nf),
        in_specs=[pl.BlockSpec((nb, tt, D_MODEL), lambda b, i, f: (b, i, 0)),
                  pl.BlockSpec((nb, 1, x_halo.shape[2], D_MODEL), lambda b, i, f: (b, i, 0, 0)),
                  pl.BlockSpec((nb, hl, tf), lambda b, i, f: (b, 0, f)),
                  pl.BlockSpec((1, D_MODEL), lambda b, i, f: (0, 0)),
                  pl.BlockSpec((None, D_MODEL, tf), lambda b, i, f: (layer, 0, f)),
                  pl.BlockSpec((None, D_MODEL, tf), lambda b, i, f: (layer, 0, f + nf)),
                  pl.BlockSpec((FFN_CONV_W, tf), lambda b, i, f: (0, f)),
                  pl.BlockSpec((1, tf), lambda b, i, f: (0, f)),
                  pl.BlockSpec((None, tf, D_MODEL), lambda b, i, f: (layer, jnp.maximum(f - 1, 0), 0)),
                  pl.BlockSpec((None, tf, D_MODEL), lambda b, i, f: (layer, nf - 1, 0))],
        out_specs=[pl.BlockSpec((nb, tt, D_MODEL), lambda b, i, f: (b, i, 0)),
                   pl.BlockSpec((nb, 1, SUBLANES, tf), lambda b, i, f: (b, i, 0, f))],
        out_shape=[jax.ShapeDtypeStruct((bsz, t, D_MODEL), F32),
                   jax.ShapeDtypeStruct((bsz, nt, SUBLANES, D_FF), F32)],
        scratch_shapes=[pltpu.VMEM((nb * (hx + tt), D_MODEL), BF16),
                        pltpu.VMEM((nb, hl + tt, tf), F32),
                        pltpu.VMEM((2, nb * tt, tf), BF16)],
        compiler_params=_params(("parallel", "parallel", "arbitrary")),
    )(x, x_halo, state, g, w_up_all, w_up_all, cw, cb, w_down_all, w_down_all)


def _final_norm_kernel(x_ref, g_ref, o_ref):
    o_ref[...] = _rms(x_ref[...], g_ref[...])


def final_norm(x2d, g, tm):
    m = x2d.shape[0]
    return pl.pallas_call(
        _final_norm_kernel,
        grid=(m // tm,),
        in_specs=[pl.BlockSpec((tm, D_MODEL), lambda i: (i, 0)),
                  pl.BlockSpec((1, D_MODEL), lambda i: (0, 0))],
        out_specs=pl.BlockSpec((tm, D_MODEL), lambda i: (i, 0)),
        out_shape=jax.ShapeDtypeStruct((m, D_MODEL), F32),
        compiler_params=_params(("parallel",)),
    )(x2d, g)


def _trunk(x, start_pos, conv_prev, pool_prev, ffn_prev, cache_k, cache_v, page_table, p, cfg):
    bsz, t, _ = x.shape
    m = bsz * t
    nb, tt = cfg["nb"], cfg["tt"]
    convs, pools, ks, vs, ffns = [], [], [], [], []
    for layer in range(DEPTH):
        li = layer // 2
        g_mix = p["norm_mix"][layer][None, :]
        x2d = x.reshape(m, D_MODEL)
        if layer % 2 == 0:
            glu, b_in = norm_proj(x2d, g_mix, p["w_in_cb"], li, 3, D_A, (F32, F32), _glu_epilogue,
                                  cfg["tm"], cfg["tn"])
            glu = glu.reshape(bsz, t, D_A)
            b_in = b_in.reshape(bsz, t, D_B)
            x = convpool_mix(glu, b_in,
                             _tile_halos(glu, conv_prev[li], tt, CONV_HALO),
                             _tile_halos(b_in, pool_prev[li], tt, POOL_HALO),
                             x, p["conv_w"][li], p["conv_b"][li][None, :], p["ln_a_g"][li][None, :],
                             p["ln_a_b"][li][None, :], p["pool_w"], p["pool_scale"][li][None, :],
                             p["w_out_cb"], li, nb, tt, start_pos)
            convs.append(jnp.concatenate([conv_prev[li], glu], axis=1)[:, -(CONV_W - 1):])
            pools.append(jnp.concatenate([pool_prev[li], b_in], axis=1)[:, -(MAX_WIN - 1):])
        else:
            q_dtype = BF16 if page_table is None else F32
            q, k, v = norm_proj(x2d, g_mix, p["w_qkv"], li, 3, D_MODEL, (q_dtype, F32, F32),
                                _qkv_epilogue, cfg["tm"], cfg["tn"])
            bias2 = p["sb_bias"][li] * LOG2E
            if page_table is None:
                o = sb_prompt_attention(q, k, v, bias2, bsz, t, cfg["tq"])
            else:
                o = sb_sample_attention(q, k, v, bias2, cache_k, cache_v, page_table, li,
                                        bsz, t, cfg["pages_per_step"])
            x = proj_res(o, p["w_o_sb"], li, x2d, cfg["tm"], cfg["tn"]).reshape(bsz, t, D_MODEL)
            ks.append(k.reshape(bsz, t, N_HEADS, HEAD_DIM))
            vs.append(v.reshape(bsz, t, N_HEADS, HEAD_DIM))
        hl = cfg["ffn_halo"]
        if cfg["ffn_recompute"]:
            x_halo = _tile_halos(x, jnp.zeros((bsz, 0, D_MODEL), F32), cfg["ffn_tt"], hl)
        else:
            x_halo = jnp.zeros((bsz, 1, SUBLANES, D_MODEL), F32)
        state = jnp.pad(ffn_prev[layer], ((0, 0), (hl - (FFN_CONV_W - 1), 0), (0, 0)))
        x, a_tail = conv_ffn(x, x_halo, state, p["norm_ffn"][layer][None, :], p["w_up"],
                             p["ffn_conv_w"][layer], p["ffn_conv_b"][layer][None, :], p["w_down"],
                             layer, cfg["ffn_nb"], cfg["ffn_tt"], cfg["tf"], hl, cfg["ffn_recompute"])
        ffns.append(a_tail[:, -1, -(FFN_CONV_W - 1):, :])
    y = final_norm(x.reshape(m, D_MODEL), p["norm_final"][None, :], cfg["tm"]).reshape(bsz, t, D_MODEL)
    return y, jnp.stack(convs), jnp.stack(pools), jnp.stack(ks), jnp.stack(vs), jnp.stack(ffns)


PROMPT_CFG = dict(nb=1, tt=256, tm=1024, tn=512, tq=512, ffn_nb=1, ffn_tt=512, tf=512, ffn_halo=16,
                  ffn_recompute=True)
SAMPLE_CFG = dict(nb=8, tt=8, tm=64, tn=512, pages_per_step=4, ffn_nb=8, ffn_tt=8, tf=512, ffn_halo=8,
                  ffn_recompute=False)


def kernel(x_prompt, x_sample, state_conv, state_pool, cache_k, cache_v, page_table, state_ffn, norm_mix, norm_ffn, norm_final, w_in_cb, conv_w, conv_b, ln_a_g, ln_a_b, pool_w, pool_scale, w_out_cb, w_qkv, w_o_sb, sb_bias, w_up, ffn_conv_w, ffn_conv_b, w_down):
    bp = x_prompt.shape[0]
    p = dict(norm_mix=norm_mix, norm_ffn=norm_ffn, norm_final=norm_final,
             w_in_cb=w_in_cb.astype(BF16), conv_w=conv_w, conv_b=conv_b, ln_a_g=ln_a_g, ln_a_b=ln_a_b,
             pool_w=pool_w.astype(BF16), pool_scale=pool_scale, w_out_cb=w_out_cb.astype(BF16),
             w_qkv=w_qkv.astype(BF16), w_o_sb=w_o_sb.astype(BF16), sb_bias=sb_bias,
             w_up=w_up.astype(BF16), ffn_conv_w=ffn_conv_w, ffn_conv_b=ffn_conv_b,
             w_down=w_down.astype(BF16))
    n_cb, n_ffn = state_conv.shape[0], state_ffn.shape[0]
    zero_conv = jnp.zeros((n_cb, bp, CONV_W - 1, D_A), F32)
    zero_pool = jnp.zeros((n_cb, bp, MAX_WIN - 1, D_B), F32)
    zero_ffn = jnp.zeros((n_ffn, bp, FFN_CONV_W - 1, D_FF), F32)
    out_p = _trunk(x_prompt, 0, zero_conv, zero_pool, zero_ffn, None, None, None, p, PROMPT_CFG)
    past_len = page_table.shape[1] * PAGE_SIZE
    out_s = _trunk(x_sample, past_len, state_conv, state_pool, state_ffn, cache_k, cache_v,
                   page_table, p, SAMPLE_CFG)
    return (out_p[0], out_s[0]) + out_p[1:] + out_s[1:]
```

```python
import functools

import jax
import jax.numpy as jnp
from jax import lax
from jax.experimental import pallas as pl
from jax.experimental.pallas import tpu as pltpu

F32 = jnp.float32
BF16 = jnp.bfloat16

D_MODEL = 2048
DEPTH = 4
PAGE_SIZE = 128
D_A = D_MODEL // 2
D_B = D_MODEL // 2
CONV_W = 31
POOL_WINDOWS = (2, 4, 8, 16)
POOL_GROUP = D_B // len(POOL_WINDOWS)
MAX_WIN = 16
N_HEADS = 16
HEAD_DIM = D_MODEL // N_HEADS
D_FF = 5632
FFN_CONV_W = 3
RMS_EPS = 1e-6
LN_EPS = 1e-5
LOG2E = 1.4426950408889634

V7X_VMEM_LIMIT_BYTES = 56 * 1024 * 1024
SUBLANES = 8
CONV_HALO = 32
POOL_HALO = 16
KEY_SUB = 256
HEADS_PER_STEP = 2


def _params(sem):
    return pltpu.CompilerParams(dimension_semantics=sem, vmem_limit_bytes=V7X_VMEM_LIMIT_BYTES)


def _rms(x, g):
    ms = jnp.mean(x * x, axis=-1, keepdims=True)
    return x * lax.rsqrt(ms + RMS_EPS) * g


def _norm_proj_kernel(x_ref, g_ref, *refs, n_w, n_prev, epilogue):
    w_refs = refs[:n_w]
    prev_refs = refs[n_w:n_w + n_prev]
    o_refs = refs[n_w + n_prev:-1]
    h_scr = refs[-1]

    @pl.when(pl.program_id(1) == 0)
    def _():
        h_scr[...] = _rms(x_ref[...], g_ref[...]).astype(BF16)

    h = h_scr[...]
    outs = epilogue(*[jnp.dot(h, w[...], preferred_element_type=F32) for w in w_refs])
    n_plain = len(o_refs) - n_prev
    for o_ref, v in zip(o_refs[:n_plain], outs):
        o_ref[...] = v.astype(o_ref.dtype)
    for o_ref, prev_ref, v in zip(o_refs[n_plain:], prev_refs, outs[n_plain:]):
        layers = prev_ref.shape[0]
        o_ref[:layers] = prev_ref[...]
        o_ref[layers] = v.astype(o_ref.dtype)


def _glu_epilogue(a_val, a_gate, b_in):
    return a_val * jax.nn.sigmoid(a_gate), b_in


def _qkv_epilogue(q, k, v):
    return q * (HEAD_DIM ** -0.5 * LOG2E), k, v


def norm_proj(x2d, g, w_all, layer, n_w, n_each, out_dtypes, epilogue, tm, tn, prevs=()):
    m = x2d.shape[0]
    nj = n_each // tn
    in_specs = [pl.BlockSpec((tm, D_MODEL), lambda i, j: (i, 0)),
                pl.BlockSpec((1, D_MODEL), lambda i, j: (0, 0))]
    for k in range(n_w):
        in_specs.append(pl.BlockSpec((None, D_MODEL, tn), lambda i, j, k=k: (layer, 0, j + k * nj)))
    n_plain = len(out_dtypes) - len(prevs)
    out_specs = [pl.BlockSpec((tm, tn), lambda i, j: (i, j)) for _ in range(n_plain)]
    out_shape = [jax.ShapeDtypeStruct((m, n_each), dt) for dt in out_dtypes[:n_plain]]
    for prev, dt in zip(prevs, out_dtypes[n_plain:]):
        layers = prev.shape[0]
        in_specs.append(pl.BlockSpec((layers, tm, tn), lambda i, j: (0, i, j)))
        out_specs.append(pl.BlockSpec((layers + 1, tm, tn), lambda i, j: (0, i, j)))
        out_shape.append(jax.ShapeDtypeStruct((layers + 1, m, n_each), dt))
    return pl.pallas_call(
        functools.partial(_norm_proj_kernel, n_w=n_w, n_prev=len(prevs), epilogue=epilogue),
        grid=(m // tm, nj),
        in_specs=in_specs,
        out_specs=out_specs,
        out_shape=out_shape,
        scratch_shapes=[pltpu.VMEM((tm, D_MODEL), BF16)],
        compiler_params=_params(("parallel", "arbitrary")),
    )(x2d, g, *([w_all] * n_w), *prevs)


def _proj_res_kernel(a_ref, w_ref, r_ref, o_ref):
    o_ref[...] = r_ref[...] + jnp.dot(a_ref[...].astype(BF16), w_ref[...], preferred_element_type=F32)


def proj_res(a2d, w_all, layer, res2d, tm, tn):
    m, kdim = a2d.shape
    n = res2d.shape[1]
    return pl.pallas_call(
        _proj_res_kernel,
        grid=(m // tm, n // tn),
        in_specs=[pl.BlockSpec((tm, kdim), lambda i, j: (i, 0)),
                  pl.BlockSpec((None, kdim, tn), lambda i, j: (layer, 0, j)),
                  pl.BlockSpec((tm, tn), lambda i, j: (i, j))],
        out_specs=pl.BlockSpec((tm, tn), lambda i, j: (i, j)),
        out_shape=jax.ShapeDtypeStruct((m, n), F32),
        compiler_params=_params(("parallel", "arbitrary")),
    )(a2d, w_all, res2d)


def _strict_lower_ones(n):
    return jnp.tril(jnp.ones((n, n), BF16), -1)


def _sb_log_terms(z2, mask):
    nz = -z2
    soft = jnp.log2(1.0 + jnp.exp2(jnp.minimum(z2, nz)))
    ls = jnp.minimum(nz, 0.0) - soft
    lsig = z2 + ls
    if mask is not None:
        ls = jnp.where(mask, ls, 0.0)
    return ls, lsig


def _sb_suffix(ls, tri):
    return jnp.dot(ls.astype(BF16), tri, preferred_element_type=F32)


def _sb_finish(lsig, rest, carry, mask):
    w = jnp.exp2(lsig + rest + carry)
    if mask is not None:
        w = jnp.where(mask, w, 0.0)
    return w


def _sb_prompt_kernel(bias_ref, tri_ref, q_ref, k_ref, v_ref, o_ref, kb_scr, vb_scr, carry_scr, acc_scr,
                      *, tq):
    hg = pl.program_id(1)
    qi = pl.program_id(2)

    @pl.when(qi == 0)
    def _():
        kb_scr[...] = k_ref[...].astype(BF16)
        vb_scr[...] = v_ref[...].astype(BF16)

    tri = tri_ref[...]
    carry_scr[...] = jnp.zeros_like(carry_scr)
    acc_scr[...] = jnp.zeros_like(acc_scr)
    n_diag = tq // KEY_SUB
    heads = range(HEADS_PER_STEP)
    lanes = [slice(j * HEAD_DIM, (j + 1) * HEAD_DIM) for j in heads]

    def block(kb, r0, masked):
        start = pl.multiple_of(kb * KEY_SUB, KEY_SUB)
        mask = None
        if masked:
            q_pos = qi * tq + r0 + lax.broadcasted_iota(jnp.int32, (tq - r0, KEY_SUB), 0)
            k_pos = kb * KEY_SUB + lax.broadcasted_iota(jnp.int32, (tq - r0, KEY_SUB), 1)
            mask = k_pos < q_pos
        zs = [lax.dot_general(q_ref[r0:, lanes[j]], kb_scr[pl.ds(start, KEY_SUB), lanes[j]],
                              (((1,), (1,)), ((), ())), preferred_element_type=F32)
              + bias_ref[hg * HEADS_PER_STEP + j] for j in heads]
        terms = [_sb_log_terms(z, mask) for z in zs]
        rests = [_sb_suffix(ls, tri) for ls, _ in terms]
        for j in heads:
            ls, lsig = terms[j]
            w = _sb_finish(lsig, rests[j], carry_scr[j, r0:, :], mask)
            carry_scr[j, r0:, :] += jnp.sum(ls, axis=-1, keepdims=True)
            acc_scr[r0:, lanes[j]] += jnp.dot(w.astype(BF16), vb_scr[pl.ds(start, KEY_SUB), lanes[j]],
                                              preferred_element_type=F32)

    for d in reversed(range(n_diag)):
        block(qi * n_diag + d, d * KEY_SUB, True)

    def body(j, c):
        block(qi * n_diag - 1 - j, 0, False)
        return c

    lax.fori_loop(0, qi * n_diag, body, 0)
    o_ref[...] = acc_scr[...].astype(o_ref.dtype)


def sb_prompt_attention(q2d, k_stack, v_stack, kl, bias2, bsz, t, tq):
    nq = t // tq
    gw = HEADS_PER_STEP * HEAD_DIM
    return pl.pallas_call(
        functools.partial(_sb_prompt_kernel, tq=tq),
        grid=(bsz, N_HEADS // HEADS_PER_STEP, nq),
        in_specs=[pl.BlockSpec(memory_space=pltpu.SMEM),
                  pl.BlockSpec((KEY_SUB, KEY_SUB), lambda b, h, i: (0, 0)),
                  pl.BlockSpec((tq, gw), lambda b, h, i: (b * nq + i, h)),
                  pl.BlockSpec((None, t, gw), lambda b, h, i: (kl, b, h)),
                  pl.BlockSpec((None, t, gw), lambda b, h, i: (kl, b, h))],
        out_specs=pl.BlockSpec((tq, gw), lambda b, h, i: (b * nq + i, h)),
        out_shape=jax.ShapeDtypeStruct((bsz * t, D_MODEL), BF16),
        scratch_shapes=[pltpu.VMEM((t, gw), BF16), pltpu.VMEM((t, gw), BF16),
                        pltpu.VMEM((HEADS_PER_STEP, tq, 1), F32), pltpu.VMEM((tq, gw), F32)],
        compiler_params=_params(("parallel", "parallel", "arbitrary")),
    )(bias2, _strict_lower_ones(KEY_SUB), q2d, k_stack, v_stack)


def _sb_sample_kernel(pt_ref, q_ref, kn_ref, vn_ref, brow_ref, tri_ref, *refs, pages_per_step, t_new):
    k_refs = refs[:pages_per_step]
    v_refs = refs[pages_per_step:2 * pages_per_step]
    o_ref = refs[2 * pages_per_step]
    carry_scr, acc_scr = refs[2 * pages_per_step + 1:]
    s = pl.program_id(1)
    rows = N_HEADS * t_new
    brow = brow_ref[...]
    head_lanes = [slice(h * HEAD_DIM, (h + 1) * HEAD_DIM) for h in range(N_HEADS)]

    def attend(key_of_head, value_of_head, tri, mask, carry):
        parts = [lax.dot_general(q_ref[:, head_lanes[h]].astype(BF16), key_of_head(h),
                                 (((1,), (1,)), ((), ())), preferred_element_type=F32)
                 for h in range(N_HEADS)]
        z = jnp.concatenate(parts, axis=0) + brow
        ls, lsig = _sb_log_terms(z, mask)
        w = _sb_finish(lsig, _sb_suffix(ls, tri), carry, mask)
        for h in range(N_HEADS):
            wh = w[h * t_new:(h + 1) * t_new, :].astype(BF16)
            acc_scr[:, head_lanes[h]] += jnp.dot(wh, value_of_head(h), preferred_element_type=F32)
        carry_scr[...] = carry + jnp.sum(ls, axis=-1, keepdims=True)

    @pl.when(s == 0)
    def _():
        acc_scr[...] = jnp.zeros_like(acc_scr)
        r = lax.broadcasted_iota(jnp.int32, (rows, PAGE_SIZE), 0)
        c = lax.broadcasted_iota(jnp.int32, (rows, PAGE_SIZE), 1)
        attend(lambda h: kn_ref[:, head_lanes[h]].astype(BF16),
               lambda h: vn_ref[:, head_lanes[h]].astype(BF16),
               tri_ref[...], c < (r % t_new), jnp.zeros((rows, 1), F32))

    @pl.when(s > 0)
    def _():
        for k_ref, v_ref in reversed(list(zip(k_refs, v_refs))):
            attend(lambda h: k_ref[pl.ds(h, PAGE_SIZE, stride=N_HEADS), :].astype(BF16),
                   lambda h: v_ref[pl.ds(h, PAGE_SIZE, stride=N_HEADS), :].astype(BF16),
                   tri_ref[...], None, carry_scr[...])

    @pl.when(s == pl.num_programs(1) - 1)
    def _():
        o_ref[...] = acc_scr[...].astype(o_ref.dtype)


def sb_sample_attention(q2d, kn2d, vn2d, bias2, cache_k, cache_v, page_table, layer, bsz, t_new,
                        pages_per_step):
    n_pages = page_table.shape[1]
    n_steps = n_pages // pages_per_step
    ck = cache_k.reshape(cache_k.shape[0], cache_k.shape[1], PAGE_SIZE * N_HEADS, HEAD_DIM)
    cv = cache_v.reshape(cache_v.shape[0], cache_v.shape[1], PAGE_SIZE * N_HEADS, HEAD_DIM)
    brow = jnp.repeat(bias2, t_new)[:, None]
    pad_new = lambda a: jnp.pad(a.reshape(bsz, t_new, D_MODEL), ((0, 0), (0, PAGE_SIZE - t_new), (0, 0)))

    def page_spec(p):
        def index_map(b, s, pt):
            page = n_pages - jnp.maximum(s, 1) * pages_per_step + p
            return (layer, pt[b * n_pages + page], 0, 0)
        return pl.BlockSpec((None, None, PAGE_SIZE * N_HEADS, HEAD_DIM), index_map)

    row_spec = pl.BlockSpec((t_new, D_MODEL), lambda b, s, pt: (b, 0))
    new_spec = pl.BlockSpec((None, PAGE_SIZE, D_MODEL), lambda b, s, pt: (b, 0, 0))
    in_specs = [row_spec, new_spec, new_spec,
                pl.BlockSpec((N_HEADS * t_new, 1), lambda b, s, pt: (0, 0)),
                pl.BlockSpec((PAGE_SIZE, PAGE_SIZE), lambda b, s, pt: (0, 0))]
    in_specs += [page_spec(p) for p in range(pages_per_step)] * 2
    return pl.pallas_call(
        functools.partial(_sb_sample_kernel, pages_per_step=pages_per_step, t_new=t_new),
        grid_spec=pltpu.PrefetchScalarGridSpec(
            num_scalar_prefetch=1,
            grid=(bsz, n_steps + 1),
            in_specs=in_specs,
            out_specs=pl.BlockSpec((t_new, D_MODEL), lambda b, s, pt: (b, 0)),
            scratch_shapes=[pltpu.VMEM((N_HEADS * t_new, 1), F32), pltpu.VMEM((t_new, D_MODEL), F32)]),
        out_shape=jax.ShapeDtypeStruct((bsz * t_new, D_MODEL), F32),
        compiler_params=_params(("parallel", "arbitrary")),
    )(page_table.reshape(-1), q2d, pad_new(kn2d), pad_new(vn2d), brow, _strict_lower_ones(PAGE_SIZE),
      *([ck] * pages_per_step), *([cv] * pages_per_step))


def _convpool_kernel(glu_ref, bin_ref, ha_ref, hb_ref, x_ref, cw_ref, cb_ref, lg_ref, lb_ref,
                     pw_ref, ps_ref, wo_ref, o_ref, ea_scr, eb_scr, sh_scr, *, nb, tt, start_pos):
    i = pl.program_id(1)
    ea_scr[:, :CONV_HALO, :] = ha_ref[:, 0]
    ea_scr[:, CONV_HALO:, :] = glu_ref[...]
    eb_scr[:, :POOL_HALO, :] = hb_ref[:, 0]
    eb_scr[:, POOL_HALO:, :] = bin_ref[...]

    first = CONV_HALO - (CONV_W - 1)
    conv = jnp.zeros((nb, tt, D_A), F32) + cb_ref[...]
    for r in range(SUBLANES):
        taps = [k for k in range(CONV_W) if (first + k) % SUBLANES == r]
        span = max(first + k for k in taps) - r + tt
        if r:
            sh_scr[r - 1, :, :span, :] = ea_scr[:, r:r + span, :]
        for k in taps:
            off = first + k - r
            rows = sh_scr[r - 1, :, off:off + tt, :] if r else ea_scr[:, off:off + tt, :]
            conv = conv + rows * cw_ref[k:k + 1, :]
    mu = jnp.mean(conv, axis=-1, keepdims=True)
    cen = conv - mu
    var = jnp.mean(cen * cen, axis=-1, keepdims=True)
    ln = cen * lax.rsqrt(var + LN_EPS) * lg_ref[...] + lb_ref[...]
    a = ln * jax.nn.sigmoid(ln)

    pos = start_pos + i * tt + lax.broadcasted_iota(jnp.int32, (nb, tt, POOL_GROUP), 1)
    ys = []
    for g, wlen in enumerate(POOL_WINDOWS):
        lanes = slice(g * POOL_GROUP, (g + 1) * POOL_GROUP)
        tok = eb_scr[:, POOL_HALO:, lanes]
        tot = tok
        for j in range(1, wlen):
            tot = tot + eb_scr[:, POOL_HALO - j:POOL_HALO - j + tt, lanes]
        cnt = jnp.minimum(wlen, pos + 1).astype(F32)
        d = (tot / cnt - tok).reshape(nb * tt, POOL_GROUP).astype(BF16)
        ys.append(jnp.dot(d, pw_ref[g], preferred_element_type=F32))
    b = jnp.concatenate(ys, axis=-1) * ps_ref[...]

    mixed = jnp.concatenate([a.reshape(nb * tt, D_A), b], axis=-1).astype(BF16)
    out = jnp.dot(mixed, wo_ref[...], preferred_element_type=F32)
    o_ref[...] = x_ref[...] + out.reshape(nb, tt, D_MODEL)


def convpool_mix(glu, b_in, halo_a, halo_b, x, cw, cb, lg, lb, pw_all, ps, wo_all, layer, nb, tt,
                 start_pos):
    bsz, t, _ = x.shape
    row = lambda c: pl.BlockSpec((nb, tt, c), lambda b, i: (b, i, 0))
    vec = lambda r, c: pl.BlockSpec((r, c), lambda b, i: (0, 0))
    return pl.pallas_call(
        functools.partial(_convpool_kernel, nb=nb, tt=tt, start_pos=start_pos),
        grid=(bsz // nb, t // tt),
        in_specs=[row(D_A), row(D_B),
                  pl.BlockSpec((nb, 1, CONV_HALO, D_A), lambda b, i: (b, i, 0, 0)),
                  pl.BlockSpec((nb, 1, POOL_HALO, D_B), lambda b, i: (b, i, 0, 0)),
                  row(D_MODEL), vec(CONV_W, D_A), vec(1, D_A), vec(1, D_A), vec(1, D_A),
                  pl.BlockSpec((None, len(POOL_WINDOWS), POOL_GROUP, POOL_GROUP),
                               lambda b, i: (layer, 0, 0, 0)),
                  vec(1, D_B),
                  pl.BlockSpec((None, D_A + D_B, D_MODEL), lambda b, i: (layer, 0, 0))],
        out_specs=row(D_MODEL),
        out_shape=jax.ShapeDtypeStruct((bsz, t, D_MODEL), F32),
        scratch_shapes=[pltpu.VMEM((nb, CONV_HALO + tt, D_A), F32),
                        pltpu.VMEM((nb, POOL_HALO + tt, D_B), F32),
                        pltpu.VMEM((SUBLANES - 1, nb, CONV_HALO - SUBLANES + tt, D_A), F32)],
        compiler_params=_params(("parallel", "arbitrary")),
    )(glu, b_in, halo_a, halo_b, x, cw, cb, lg, lb, pw_all, ps, wo_all)


def _tile_halos(stream, state, tt, halo):
    bsz, t, c = stream.shape
    first = jnp.pad(state, ((0, 0), (halo - state.shape[1], 0), (0, 0)))[:, None]
    if t == tt:
        return first
    rest = stream.reshape(bsz, t // tt, tt, c)[:, :-1, tt - halo:, :]
    return jnp.concatenate([first, rest], axis=1)


def _ffn_kernel(x_ref, xh_ref, st_ref, g_ref, wa_ref, wg_ref, cw_ref, cb_ref, wdp_ref, wdl_ref, *refs,
                nb, tt, hl, recompute, final_norm):
    fg_ref = refs[0] if final_norm else None
    o_ref, at_ref, h_scr, a_scr, y_scr = refs[1:] if final_norm else refs
    i = pl.program_id(1)
    f = pl.program_id(2)
    nf = pl.num_programs(2)
    hx = hl if recompute else 0
    tf = wa_ref.shape[-1]

    @pl.when(f == 0)
    def _():
        x = x_ref[...]
        h_scr[hx:, :] = _rms(x, g_ref[...]).reshape(nb * tt, D_MODEL).astype(BF16)
        if recompute:
            h_scr[:hx, :] = _rms(xh_ref[0, 0], g_ref[...]).astype(BF16)
        o_ref[...] = x
        y_scr[1] = jnp.zeros(y_scr.shape[1:], BF16)

    h = h_scr[...]
    a_all = jnp.dot(h, wa_ref[...], preferred_element_type=F32).reshape(nb, hx + tt, tf)
    gate = jnp.dot(h, wg_ref[...], preferred_element_type=F32).reshape(nb, hx + tt, tf)[:, hx:, :]
    o_ref[...] += jnp.dot(y_scr[(f + 1) % 2], wdp_ref[...],
                          preferred_element_type=F32).reshape(nb, tt, D_MODEL)
    if recompute:
        a_scr[:, :hl, :] = jnp.where(i == 0, st_ref[...], a_all[:, :hl, :])
        a_scr[:, hl:, :] = a_all[:, hl:, :]
    else:
        a_scr[:, :hl, :] = st_ref[...]
        a_scr[:, hl:, :] = a_all

    conv = cb_ref[...] + a_scr[:, hl:, :] * cw_ref[2:3, :]
    conv = conv + a_scr[:, hl - 1:hl - 1 + tt, :] * cw_ref[1:2, :]
    conv = conv + a_scr[:, hl - 2:hl - 2 + tt, :] * cw_ref[0:1, :]
    y_scr[f % 2] = (jax.nn.gelu(conv) * gate).reshape(nb * tt, tf).astype(BF16)
    at_ref[:, 0] = a_scr[:, hl + tt - SUBLANES:, :]

    @pl.when(f == nf - 1)
    def _():
        out = o_ref[...] + jnp.dot(y_scr[f % 2], wdl_ref[...],
                                   preferred_element_type=F32).reshape(nb, tt, D_MODEL)
        o_ref[...] = _rms(out, fg_ref[...]) if final_norm else out


def conv_ffn(x, x_halo, state, g, w_up_all, cw, cb, w_down_all, layer, nb, tt, tf, hl, recompute,
             final_g=None):
    bsz, t, _ = x.shape
    nt = t // tt
    nf = D_FF // tf
    hx = hl if recompute else 0
    final_norm = final_g is not None
    extra_specs = [pl.BlockSpec((1, D_MODEL), lambda b, i, f: (0, 0))] if final_norm else []
    extra_args = [final_g] if final_norm else []
    return pl.pallas_call(
        functools.partial(_ffn_kernel, nb=nb, tt=tt, hl=hl, recompute=recompute, final_norm=final_norm),
        grid=(bsz // nb, nt, nf),
        in_specs=[pl.BlockSpec((nb, tt, D_MODEL), lambda b, i, f: (b, i, 0)),
                  pl.BlockSpec((nb, 1, x_halo.shape[2], D_MODEL), lambda b, i, f: (b, i, 0, 0)),
                  pl.BlockSpec((nb, hl, tf), lambda b, i, f: (b, 0, f)),
                  pl.BlockSpec((1, D_MODEL), lambda b, i, f: (0, 0)),
                  pl.BlockSpec((None, D_MODEL, tf), lambda b, i, f: (layer, 0, f)),
                  pl.BlockSpec((None, D_MODEL, tf), lambda b, i, f: (layer, 0, f + nf)),
                  pl.BlockSpec((FFN_CONV_W, tf), lambda b, i, f: (0, f)),
                  pl.BlockSpec((1, tf), lambda b, i, f: (0, f)),
                  pl.BlockSpec((None, tf, D_MODEL), lambda b, i, f: (layer, jnp.maximum(f - 1, 0), 0)),
                  pl.BlockSpec((None, tf, D_MODEL), lambda b, i, f: (layer, nf - 1, 0))] + extra_specs,
        out_specs=[pl.BlockSpec((nb, tt, D_MODEL), lambda b, i, f: (b, i, 0)),
                   pl.BlockSpec((nb, 1, SUBLANES, tf), lambda b, i, f: (b, i, 0, f))],
        out_shape=[jax.ShapeDtypeStruct((bsz, t, D_MODEL), F32),
                   jax.ShapeDtypeStruct((bsz, nt, SUBLANES, D_FF), F32)],
        scratch_shapes=[pltpu.VMEM((nb * (hx + tt), D_MODEL), BF16),
                        pltpu.VMEM((nb, hl + tt, tf), F32),
                        pltpu.VMEM((2, nb * tt, tf), BF16)],
        compiler_params=_params(("parallel", "parallel", "arbitrary")),
    )(x, x_halo, state, g, w_up_all, w_up_all, cw, cb, w_down_all, w_down_all, *extra_args)


def _trunk(x, start_pos, conv_prev, pool_prev, ffn_prev, cache_k, cache_v, page_table, p, cfg):
    bsz, t, _ = x.shape
    m = bsz * t
    nb, tt = cfg["nb"], cfg["tt"]
    convs, pools, ffns = [], [], []
    kv_stacks = ()
    for layer in range(DEPTH):
        li = layer // 2
        g_mix = p["norm_mix"][layer][None, :]
        x2d = x.reshape(m, D_MODEL)
        if layer % 2 == 0:
            glu, b_in = norm_proj(x2d, g_mix, p["w_in_cb"], li, 3, D_A, (F32, F32), _glu_epilogue,
                                  cfg["tm"], cfg["tn"])
            glu = glu.reshape(bsz, t, D_A)
            b_in = b_in.reshape(bsz, t, D_B)
            x = convpool_mix(glu, b_in,
                             _tile_halos(glu, conv_prev[li], tt, CONV_HALO),
                             _tile_halos(b_in, pool_prev[li], tt, POOL_HALO),
                             x, p["conv_w"][li], p["conv_b"][li][None, :], p["ln_a_g"][li][None, :],
                             p["ln_a_b"][li][None, :], p["pool_w"], p["pool_scale"][li][None, :],
                             p["w_out_cb"], li, nb, tt, start_pos)
            convs.append(jnp.concatenate([conv_prev[li], glu], axis=1)[:, -(CONV_W - 1):])
            pools.append(jnp.concatenate([pool_prev[li], b_in], axis=1)[:, -(MAX_WIN - 1):])
        else:
            q_dtype = BF16 if page_table is None else F32
            tn_qkv = cfg["tn"] // 2 if kv_stacks else cfg["tn"]
            q, k_stack, v_stack = norm_proj(x2d, g_mix, p["w_qkv"], li, 3, D_MODEL, (q_dtype, F32, F32),
                                            _qkv_epilogue, cfg["tm"], tn_qkv, prevs=kv_stacks)
            if not kv_stacks:
                k_stack, v_stack = k_stack[None], v_stack[None]
            kv_stacks = (k_stack, v_stack)
            bias2 = p["sb_bias"][li] * LOG2E
            if page_table is None:
                o = sb_prompt_attention(q, k_stack, v_stack, li, bias2, bsz, t, cfg["tq"])
            else:
                o = sb_sample_attention(q, k_stack[li], v_stack[li], bias2, cache_k, cache_v, page_table,
                                        li, bsz, t, cfg["pages_per_step"])
            x = proj_res(o, p["w_o_sb"], li, x2d, cfg["tm"], cfg["tn"]).reshape(bsz, t, D_MODEL)
        hl = cfg["ffn_halo"]
        if cfg["ffn_recompute"]:
            x_halo = _tile_halos(x, jnp.zeros((bsz, 0, D_MODEL), F32), cfg["ffn_tt"], hl)
        else:
            x_halo = jnp.zeros((bsz, 1, SUBLANES, D_MODEL), F32)
        state = jnp.pad(ffn_prev[layer], ((0, 0), (hl - (FFN_CONV_W - 1), 0), (0, 0)))
        final_g = p["norm_final"][None, :] if layer == DEPTH - 1 else None
        x, a_tail = conv_ffn(x, x_halo, state, p["norm_ffn"][layer][None, :], p["w_up"],
                             p["ffn_conv_w"][layer], p["ffn_conv_b"][layer][None, :], p["w_down"],
                             layer, cfg["ffn_nb"], cfg["ffn_tt"], cfg["tf"], hl, cfg["ffn_recompute"],
                             final_g)
        ffns.append(a_tail[:, -1, -(FFN_CONV_W - 1):, :])
    k_stack, v_stack = (a.reshape(a.shape[0], bsz, t, N_HEADS, HEAD_DIM) for a in kv_stacks)
    return x, jnp.stack(convs), jnp.stack(pools), k_stack, v_stack, jnp.stack(ffns)


PROMPT_CFG = dict(nb=1, tt=256, tm=1024, tn=512, tq=512, ffn_nb=1, ffn_tt=512, tf=512, ffn_halo=16,
                  ffn_recompute=True)
SAMPLE_CFG = dict(nb=8, tt=8, tm=64, tn=512, pages_per_step=8, ffn_nb=8, ffn_tt=8, tf=512, ffn_halo=8,
                  ffn_recompute=False)


def kernel(x_prompt, x_sample, state_conv, state_pool, cache_k, cache_v, page_table, state_ffn, norm_mix, norm_ffn, norm_final, w_in_cb, conv_w, conv_b, ln_a_g, ln_a_b, pool_w, pool_scale, w_out_cb, w_qkv, w_o_sb, sb_bias, w_up, ffn_conv_w, ffn_conv_b, w_down):
    bp = x_prompt.shape[0]
    p = dict(norm_mix=norm_mix, norm_ffn=norm_ffn, norm_final=norm_final,
             w_in_cb=w_in_cb.astype(BF16), conv_w=conv_w, conv_b=conv_b, ln_a_g=ln_a_g, ln_a_b=ln_a_b,
             pool_w=pool_w.astype(BF16), pool_scale=pool_scale, w_out_cb=w_out_cb.astype(BF16),
             w_qkv=w_qkv.astype(BF16), w_o_sb=w_o_sb.astype(BF16), sb_bias=sb_bias,
             w_up=w_up.astype(BF16), ffn_conv_w=ffn_conv_w, ffn_conv_b=ffn_conv_b,
             w_down=w_down.astype(BF16))
    n_cb, n_ffn = state_conv.shape[0], state_ffn.shape[0]
    zero_conv = jnp.zeros((n_cb, bp, CONV_W - 1, D_A), F32)
    zero_pool = jnp.zeros((n_cb, bp, MAX_WIN - 1, D_B), F32)
    zero_ffn = jnp.zeros((n_ffn, bp, FFN_CONV_W - 1, D_FF), F32)
    out_p = _trunk(x_prompt, 0, zero_conv, zero_pool, zero_ffn, None, None, None, p, PROMPT_CFG)
    past_len = page_table.shape[1] * PAGE_SIZE
    out_s = _trunk(x_sample, past_len, state_conv, state_pool, state_ffn, cache_k, cache_v,
                   page_table, p, SAMPLE_CFG)
    return (out_p[0], out_s[0]) + out_p[1:] + out_s[1:]
```

```python
import functools

import jax
import jax.numpy as jnp
from jax import lax
from jax.experimental import pallas as pl
from jax.experimental.pallas import tpu as pltpu

F32 = jnp.float32
BF16 = jnp.bfloat16

D_MODEL = 2048
DEPTH = 4
PAGE_SIZE = 128
D_A = D_MODEL // 2
D_B = D_MODEL // 2
CONV_W = 31
POOL_WINDOWS = (2, 4, 8, 16)
POOL_GROUP = D_B // len(POOL_WINDOWS)
MAX_WIN = 16
N_HEADS = 16
HEAD_DIM = D_MODEL // N_HEADS
D_FF = 5632
FFN_CONV_W = 3
RMS_EPS = 1e-6
LN_EPS = 1e-5
LOG2E = 1.4426950408889634

V7X_VMEM_LIMIT_BYTES = 56 * 1024 * 1024
SUBLANES = 8
CONV_HALO = 32
POOL_HALO = 16
KEY_SUB = 256
HEADS_PER_STEP = 2


def _params(sem):
    return pltpu.CompilerParams(dimension_semantics=sem, vmem_limit_bytes=V7X_VMEM_LIMIT_BYTES)


def _rms(x, g):
    ms = jnp.mean(x * x, axis=-1, keepdims=True)
    return x * lax.rsqrt(ms + RMS_EPS) * g


def _norm_proj_kernel(x_ref, g_ref, *refs, n_w, n_prev, epilogue):
    w_refs = refs[:n_w]
    prev_refs = refs[n_w:n_w + n_prev]
    o_refs = refs[n_w + n_prev:-1]
    h_scr = refs[-1]

    @pl.when(pl.program_id(1) == 0)
    def _():
        h_scr[...] = _rms(x_ref[...], g_ref[...]).astype(BF16)

    h = h_scr[...]
    outs = epilogue(*[jnp.dot(h, w[...], preferred_element_type=F32) for w in w_refs])
    n_plain = len(o_refs) - n_prev
    for o_ref, v in zip(o_refs[:n_plain], outs):
        o_ref[...] = v.astype(o_ref.dtype)
    for o_ref, prev_ref, v in zip(o_refs[n_plain:], prev_refs, outs[n_plain:]):
        layers = prev_ref.shape[0]
        o_ref[:layers] = prev_ref[...]
        o_ref[layers] = v.astype(o_ref.dtype)


def _glu_epilogue(a_val, a_gate, b_in):
    return a_val * jax.nn.sigmoid(a_gate), b_in


def _qkv_epilogue(q, k, v):
    return q * (HEAD_DIM ** -0.5 * LOG2E), k, v


def norm_proj(x2d, g, w_all, layer, n_w, n_each, out_dtypes, epilogue, tm, tn, prevs=()):
    m = x2d.shape[0]
    nj = n_each // tn
    in_specs = [pl.BlockSpec((tm, D_MODEL), lambda i, j: (i, 0)),
                pl.BlockSpec((1, D_MODEL), lambda i, j: (0, 0))]
    for k in range(n_w):
        in_specs.append(pl.BlockSpec((None, D_MODEL, tn), lambda i, j, k=k: (layer, 0, j + k * nj)))
    n_plain = len(out_dtypes) - len(prevs)
    out_specs = [pl.BlockSpec((tm, tn), lambda i, j: (i, j)) for _ in range(n_plain)]
    out_shape = [jax.ShapeDtypeStruct((m, n_each), dt) for dt in out_dtypes[:n_plain]]
    for prev, dt in zip(prevs, out_dtypes[n_plain:]):
        layers = prev.shape[0]
        in_specs.append(pl.BlockSpec((layers, tm, tn), lambda i, j: (0, i, j)))
        out_specs.append(pl.BlockSpec((layers + 1, tm, tn), lambda i, j: (0, i, j)))
        out_shape.append(jax.ShapeDtypeStruct((layers + 1, m, n_each), dt))
    return pl.pallas_call(
        functools.partial(_norm_proj_kernel, n_w=n_w, n_prev=len(prevs), epilogue=epilogue),
        grid=(m // tm, nj),
        in_specs=in_specs,
        out_specs=out_specs,
        out_shape=out_shape,
        scratch_shapes=[pltpu.VMEM((tm, D_MODEL), BF16)],
        compiler_params=_params(("parallel", "arbitrary")),
    )(x2d, g, *([w_all] * n_w), *prevs)


def _proj_res_kernel(a_ref, w_ref, r_ref, o_ref):
    o_ref[...] = r_ref[...] + jnp.dot(a_ref[...].astype(BF16), w_ref[...], preferred_element_type=F32)


def proj_res(a2d, w_all, layer, res2d, tm, tn):
    m, kdim = a2d.shape
    n = res2d.shape[1]
    return pl.pallas_call(
        _proj_res_kernel,
        grid=(m // tm, n // tn),
        in_specs=[pl.BlockSpec((tm, kdim), lambda i, j: (i, 0)),
                  pl.BlockSpec((None, kdim, tn), lambda i, j: (layer, 0, j)),
                  pl.BlockSpec((tm, tn), lambda i, j: (i, j))],
        out_specs=pl.BlockSpec((tm, tn), lambda i, j: (i, j)),
        out_shape=jax.ShapeDtypeStruct((m, n), F32),
        compiler_params=_params(("parallel", "arbitrary")),
    )(a2d, w_all, res2d)


def _strict_lower_ones(n):
    return jnp.tril(jnp.ones((n, n), BF16), -1)


def _sb_log_terms(z2, mask):
    nz = -z2
    soft = jnp.log2(1.0 + jnp.exp2(jnp.minimum(z2, nz)))
    ls = jnp.minimum(nz, 0.0) - soft
    lsig = z2 + ls
    if mask is not None:
        ls = jnp.where(mask, ls, 0.0)
    return ls, lsig


def _sb_suffix(ls, tri):
    return jnp.dot(ls.astype(BF16), tri, preferred_element_type=F32)


def _sb_finish(lsig, rest, carry, mask):
    w = jnp.exp2(lsig + rest + carry)
    if mask is not None:
        w = jnp.where(mask, w, 0.0)
    return w


def _sb_prompt_kernel(bias_ref, tri_ref, q_ref, k_ref, v_ref, o_ref, kb_scr, vb_scr, carry_scr, acc_scr,
                      *, tq):
    hg = pl.program_id(1)
    qi = pl.program_id(2)

    @pl.when(qi == 0)
    def _():
        kb_scr[...] = k_ref[...].astype(BF16)
        vb_scr[...] = v_ref[...].astype(BF16)

    tri = tri_ref[...]
    carry_scr[...] = jnp.zeros_like(carry_scr)
    acc_scr[...] = jnp.zeros_like(acc_scr)
    n_diag = tq // KEY_SUB
    heads = range(HEADS_PER_STEP)
    lanes = [slice(j * HEAD_DIM, (j + 1) * HEAD_DIM) for j in heads]

    def block(kb, r0, masked):
        start = pl.multiple_of(kb * KEY_SUB, KEY_SUB)
        mask = None
        if masked:
            q_pos = qi * tq + r0 + lax.broadcasted_iota(jnp.int32, (tq - r0, KEY_SUB), 0)
            k_pos = kb * KEY_SUB + lax.broadcasted_iota(jnp.int32, (tq - r0, KEY_SUB), 1)
            mask = k_pos < q_pos
        zs = [lax.dot_general(q_ref[r0:, lanes[j]], kb_scr[pl.ds(start, KEY_SUB), lanes[j]],
                              (((1,), (1,)), ((), ())), preferred_element_type=F32)
              + bias_ref[hg * HEADS_PER_STEP + j] for j in heads]
        terms = [_sb_log_terms(z, mask) for z in zs]
        rests = [_sb_suffix(ls, tri) for ls, _ in terms]
        for j in heads:
            ls, lsig = terms[j]
            w = _sb_finish(lsig, rests[j], carry_scr[j, r0:, :], mask)
            carry_scr[j, r0:, :] += jnp.sum(ls, axis=-1, keepdims=True)
            acc_scr[r0:, lanes[j]] += jnp.dot(w.astype(BF16), vb_scr[pl.ds(start, KEY_SUB), lanes[j]],
                                              preferred_element_type=F32)

    for d in reversed(range(n_diag)):
        block(qi * n_diag + d, d * KEY_SUB, True)

    def body(j, c):
        block(qi * n_diag - 1 - j, 0, False)
        return c

    lax.fori_loop(0, qi * n_diag, body, 0)
    o_ref[...] = acc_scr[...].astype(o_ref.dtype)


def sb_prompt_attention(q2d, k_stack, v_stack, kl, bias2, bsz, t, tq):
    nq = t // tq
    gw = HEADS_PER_STEP * HEAD_DIM
    return pl.pallas_call(
        functools.partial(_sb_prompt_kernel, tq=tq),
        grid=(bsz, N_HEADS // HEADS_PER_STEP, nq),
        in_specs=[pl.BlockSpec(memory_space=pltpu.SMEM),
                  pl.BlockSpec((KEY_SUB, KEY_SUB), lambda b, h, i: (0, 0)),
                  pl.BlockSpec((tq, gw), lambda b, h, i: (b * nq + i, h)),
                  pl.BlockSpec((None, t, gw), lambda b, h, i: (kl, b, h)),
                  pl.BlockSpec((None, t, gw), lambda b, h, i: (kl, b, h))],
        out_specs=pl.BlockSpec((tq, gw), lambda b, h, i: (b * nq + i, h)),
        out_shape=jax.ShapeDtypeStruct((bsz * t, D_MODEL), BF16),
        scratch_shapes=[pltpu.VMEM((t, gw), BF16), pltpu.VMEM((t, gw), BF16),
                        pltpu.VMEM((HEADS_PER_STEP, tq, 1), F32), pltpu.VMEM((tq, gw), F32)],
        compiler_params=_params(("parallel", "parallel", "arbitrary")),
    )(bias2, _strict_lower_ones(KEY_SUB), q2d, k_stack, v_stack)


def _sb_sample_kernel(pt_ref, q_ref, kn_ref, vn_ref, brow_ref, tri_ref, *refs, pages_per_step, t_new):
    k_refs = refs[:pages_per_step]
    v_refs = refs[pages_per_step:2 * pages_per_step]
    o_ref = refs[2 * pages_per_step]
    carry_scr, acc_scr = refs[2 * pages_per_step + 1:]
    s = pl.program_id(1)
    rows = N_HEADS * t_new
    brow = brow_ref[...]
    head_lanes = [slice(h * HEAD_DIM, (h + 1) * HEAD_DIM) for h in range(N_HEADS)]

    def attend(key_of_head, value_of_head, tri, mask, carry):
        parts = [lax.dot_general(q_ref[:, head_lanes[h]].astype(BF16), key_of_head(h),
                                 (((1,), (1,)), ((), ())), preferred_element_type=F32)
                 for h in range(N_HEADS)]
        z = jnp.concatenate(parts, axis=0) + brow
        ls, lsig = _sb_log_terms(z, mask)
        w = _sb_finish(lsig, _sb_suffix(ls, tri), carry, mask)
        for h in range(N_HEADS):
            wh = w[h * t_new:(h + 1) * t_new, :].astype(BF16)
            acc_scr[:, head_lanes[h]] += jnp.dot(wh, value_of_head(h), preferred_element_type=F32)
        carry_scr[...] = carry + jnp.sum(ls, axis=-1, keepdims=True)

    @pl.when(s == 0)
    def _():
        acc_scr[...] = jnp.zeros_like(acc_scr)
        r = lax.broadcasted_iota(jnp.int32, (rows, PAGE_SIZE), 0)
        c = lax.broadcasted_iota(jnp.int32, (rows, PAGE_SIZE), 1)
        attend(lambda h: kn_ref[:, head_lanes[h]].astype(BF16),
               lambda h: vn_ref[:, head_lanes[h]].astype(BF16),
               tri_ref[...], c < (r % t_new), jnp.zeros((rows, 1), F32))

    @pl.when(s > 0)
    def _():
        for k_ref, v_ref in reversed(list(zip(k_refs, v_refs))):
            attend(lambda h: k_ref[pl.ds(h, PAGE_SIZE, stride=N_HEADS), :].astype(BF16),
                   lambda h: v_ref[pl.ds(h, PAGE_SIZE, stride=N_HEADS), :].astype(BF16),
                   tri_ref[...], None, carry_scr[...])

    @pl.when(s == pl.num_programs(1) - 1)
    def _():
        o_ref[...] = acc_scr[...].astype(o_ref.dtype)


def sb_sample_attention(q2d, kn2d, vn2d, bias2, cache_k, cache_v, page_table, layer, bsz, t_new,
                        pages_per_step):
    n_pages = page_table.shape[1]
    n_steps = n_pages // pages_per_step
    ck = cache_k.reshape(cache_k.shape[0], cache_k.shape[1], PAGE_SIZE * N_HEADS, HEAD_DIM)
    cv = cache_v.reshape(cache_v.shape[0], cache_v.shape[1], PAGE_SIZE * N_HEADS, HEAD_DIM)
    brow = jnp.repeat(bias2, t_new)[:, None]
    pad_new = lambda a: jnp.pad(a.reshape(bsz, t_new, D_MODEL), ((0, 0), (0, PAGE_SIZE - t_new), (0, 0)))

    def page_spec(p):
        def index_map(b, s, pt):
            page = n_pages - jnp.maximum(s, 1) * pages_per_step + p
            return (layer, pt[b * n_pages + page], 0, 0)
        return pl.BlockSpec((None, None, PAGE_SIZE * N_HEADS, HEAD_DIM), index_map)

    row_spec = pl.BlockSpec((t_new, D_MODEL), lambda b, s, pt: (b, 0))
    new_spec = pl.BlockSpec((None, PAGE_SIZE, D_MODEL), lambda b, s, pt: (b, 0, 0))
    in_specs = [row_spec, new_spec, new_spec,
                pl.BlockSpec((N_HEADS * t_new, 1), lambda b, s, pt: (0, 0)),
                pl.BlockSpec((PAGE_SIZE, PAGE_SIZE), lambda b, s, pt: (0, 0))]
    in_specs += [page_spec(p) for p in range(pages_per_step)] * 2
    return pl.pallas_call(
        functools.partial(_sb_sample_kernel, pages_per_step=pages_per_step, t_new=t_new),
        grid_spec=pltpu.PrefetchScalarGridSpec(
            num_scalar_prefetch=1,
            grid=(bsz, n_steps + 1),
            in_specs=in_specs,
            out_specs=pl.BlockSpec((t_new, D_MODEL), lambda b, s, pt: (b, 0)),
            scratch_shapes=[pltpu.VMEM((N_HEADS * t_new, 1), F32), pltpu.VMEM((t_new, D_MODEL), F32)]),
        out_shape=jax.ShapeDtypeStruct((bsz * t_new, D_MODEL), F32),
        compiler_params=_params(("parallel", "arbitrary")),
    )(page_table.reshape(-1), q2d, pad_new(kn2d), pad_new(vn2d), brow, _strict_lower_ones(PAGE_SIZE),
      *([ck] * pages_per_step), *([cv] * pages_per_step))


def _convpool_kernel(glu_ref, bin_ref, ha_ref, hb_ref, x_ref, cw_ref, cb_ref, lg_ref, lb_ref,
                     pw_ref, ps_ref, wo_ref, o_ref, ea_scr, eb_scr, sh_scr, *, nb, tt, start_pos):
    i = pl.program_id(1)
    ea_scr[:, :CONV_HALO, :] = ha_ref[:, 0]
    ea_scr[:, CONV_HALO:, :] = glu_ref[...]
    eb_scr[:, :POOL_HALO, :] = hb_ref[:, 0]
    eb_scr[:, POOL_HALO:, :] = bin_ref[...]

    first = CONV_HALO - (CONV_W - 1)
    conv = jnp.zeros((nb, tt, D_A), F32) + cb_ref[...]
    for r in range(SUBLANES):
        taps = [k for k in range(CONV_W) if (first + k) % SUBLANES == r]
        span = max(first + k for k in taps) - r + tt
        if r:
            sh_scr[r - 1, :, :span, :] = ea_scr[:, r:r + span, :]
        for k in taps:
            off = first + k - r
            rows = sh_scr[r - 1, :, off:off + tt, :] if r else ea_scr[:, off:off + tt, :]
            conv = conv + rows * cw_ref[k:k + 1, :]
    mu = jnp.mean(conv, axis=-1, keepdims=True)
    cen = conv - mu
    var = jnp.mean(cen * cen, axis=-1, keepdims=True)
    ln = cen * lax.rsqrt(var + LN_EPS) * lg_ref[...] + lb_ref[...]
    a = ln * jax.nn.sigmoid(ln)

    pos = start_pos + i * tt + lax.broadcasted_iota(jnp.int32, (nb, tt, POOL_GROUP), 1)
    ys = []
    for g, wlen in enumerate(POOL_WINDOWS):
        lanes = slice(g * POOL_GROUP, (g + 1) * POOL_GROUP)
        tok = eb_scr[:, POOL_HALO:, lanes]
        tot = tok
        for j in range(1, wlen):
            tot = tot + eb_scr[:, POOL_HALO - j:POOL_HALO - j + tt, lanes]
        cnt = jnp.minimum(wlen, pos + 1).astype(F32)
        d = (tot / cnt - tok).reshape(nb * tt, POOL_GROUP).astype(BF16)
        ys.append(jnp.dot(d, pw_ref[g], preferred_element_type=F32))
    b = jnp.concatenate(ys, axis=-1) * ps_ref[...]

    mixed = jnp.concatenate([a.reshape(nb * tt, D_A), b], axis=-1).astype(BF16)
    out = jnp.dot(mixed, wo_ref[...], preferred_element_type=F32)
    o_ref[...] = x_ref[...] + out.reshape(nb, tt, D_MODEL)


def convpool_mix(glu, b_in, halo_a, halo_b, x, cw, cb, lg, lb, pw_all, ps, wo_all, layer, nb, tt,
                 start_pos):
    bsz, t, _ = x.shape
    row = lambda c: pl.BlockSpec((nb, tt, c), lambda b, i: (b, i, 0))
    vec = lambda r, c: pl.BlockSpec((r, c), lambda b, i: (0, 0))
    return pl.pallas_call(
        functools.partial(_convpool_kernel, nb=nb, tt=tt, start_pos=start_pos),
        grid=(bsz // nb, t // tt),
        in_specs=[row(D_A), row(D_B),
                  pl.BlockSpec((nb, 1, CONV_HALO, D_A), lambda b, i: (b, i, 0, 0)),
                  pl.BlockSpec((nb, 1, POOL_HALO, D_B), lambda b, i: (b, i, 0, 0)),
                  row(D_MODEL), vec(CONV_W, D_A), vec(1, D_A), vec(1, D_A), vec(1, D_A),
                  pl.BlockSpec((None, len(POOL_WINDOWS), POOL_GROUP, POOL_GROUP),
                               lambda b, i: (layer, 0, 0, 0)),
                  vec(1, D_B),
                  pl.BlockSpec((None, D_A + D_B, D_MODEL), lambda b, i: (layer, 0, 0))],
        out_specs=row(D_MODEL),
        out_shape=jax.ShapeDtypeStruct((bsz, t, D_MODEL), F32),
        scratch_shapes=[pltpu.VMEM((nb, CONV_HALO + tt, D_A), F32),
                        pltpu.VMEM((nb, POOL_HALO + tt, D_B), F32),
                        pltpu.VMEM((SUBLANES - 1, nb, CONV_HALO - SUBLANES + tt, D_A), F32)],
        compiler_params=_params(("parallel", "arbitrary")),
    )(glu, b_in, halo_a, halo_b, x, cw, cb, lg, lb, pw_all, ps, wo_all)


def _tile_halos(stream, state, tt, halo):
    bsz, t, c = stream.shape
    first = jnp.pad(state, ((0, 0), (halo - state.shape[1], 0), (0, 0)))[:, None]
    if t == tt:
        return first
    rest = stream.reshape(bsz, t // tt, tt, c)[:, :-1, tt - halo:, :]
    return jnp.concatenate([first, rest], axis=1)


def _ffn_kernel(x_ref, xh_ref, st_ref, g_ref, wa_ref, wg_ref, cw_ref, cb_ref, wdp_ref, wdl_ref, *refs,
                nb, tt, hl, recompute, final_norm):
    fg_ref = refs[0] if final_norm else None
    o_ref, at_ref, h_scr, a_scr, y_scr = refs[1:] if final_norm else refs
    i = pl.program_id(1)
    f = pl.program_id(2)
    nf = pl.num_programs(2)
    hx = hl if recompute else 0
    tf = wa_ref.shape[-1]

    @pl.when(f == 0)
    def _():
        x = x_ref[...]
        h_scr[hx:, :] = _rms(x, g_ref[...]).reshape(nb * tt, D_MODEL).astype(BF16)
        if recompute:
            h_scr[:hx, :] = _rms(xh_ref[0, 0], g_ref[...]).astype(BF16)
        o_ref[...] = x
        y_scr[1] = jnp.zeros(y_scr.shape[1:], BF16)

    h = h_scr[...]
    a_all = jnp.dot(h, wa_ref[...], preferred_element_type=F32).reshape(nb, hx + tt, tf)
    gate = jnp.dot(h, wg_ref[...], preferred_element_type=F32).reshape(nb, hx + tt, tf)[:, hx:, :]
    o_ref[...] += jnp.dot(y_scr[(f + 1) % 2], wdp_ref[...],
                          preferred_element_type=F32).reshape(nb, tt, D_MODEL)
    if recompute:
        a_scr[:, :hl, :] = jnp.where(i == 0, st_ref[...], a_all[:, :hl, :])
        a_scr[:, hl:, :] = a_all[:, hl:, :]
    else:
        a_scr[:, :hl, :] = st_ref[...]
        a_scr[:, hl:, :] = a_all

    conv = cb_ref[...] + a_scr[:, hl:, :] * cw_ref[2:3, :]
    conv = conv + a_scr[:, hl - 1:hl - 1 + tt, :] * cw_ref[1:2, :]
    conv = conv + a_scr[:, hl - 2:hl - 2 + tt, :] * cw_ref[0:1, :]
    y_scr[f % 2] = (jax.nn.gelu(conv) * gate).reshape(nb * tt, tf).astype(BF16)
    at_ref[:, 0] = a_scr[:, hl + tt - SUBLANES:, :]

    @pl.when(f == nf - 1)
    def _():
        out = o_ref[...] + jnp.dot(y_scr[f % 2], wdl_ref[...],
                                   preferred_element_type=F32).reshape(nb, tt, D_MODEL)
        o_ref[...] = _rms(out, fg_ref[...]) if final_norm else out


def conv_ffn(x, x_halo, state, g, w_up_all, cw, cb, w_down_all, layer, nb, tt, tf, hl, recompute,
             final_g=None):
    assert w_up_all.shape[-1] == tf
    bsz, t, _ = x.shape
    nt = t // tt
    nf = D_FF // tf
    hx = hl if recompute else 0
    final_norm = final_g is not None
    extra_specs = [pl.BlockSpec((1, D_MODEL), lambda b, i, f: (0, 0))] if final_norm else []
    extra_args = [final_g] if final_norm else []
    return pl.pallas_call(
        functools.partial(_ffn_kernel, nb=nb, tt=tt, hl=hl, recompute=recompute, final_norm=final_norm),
        grid=(bsz // nb, nt, nf),
        in_specs=[pl.BlockSpec((nb, tt, D_MODEL), lambda b, i, f: (b, i, 0)),
                  pl.BlockSpec((nb, 1, x_halo.shape[2], D_MODEL), lambda b, i, f: (b, i, 0, 0)),
                  pl.BlockSpec((nb, hl, tf), lambda b, i, f: (b, 0, f)),
                  pl.BlockSpec((1, D_MODEL), lambda b, i, f: (0, 0)),
                  pl.BlockSpec((None, None, D_MODEL, tf), lambda b, i, f: (layer, f, 0, 0)),
                  pl.BlockSpec((None, None, D_MODEL, tf), lambda b, i, f: (layer, f + nf, 0, 0)),
                  pl.BlockSpec((FFN_CONV_W, tf), lambda b, i, f: (0, f)),
                  pl.BlockSpec((1, tf), lambda b, i, f: (0, f)),
                  pl.BlockSpec((None, tf, D_MODEL), lambda b, i, f: (layer, jnp.maximum(f - 1, 0), 0)),
                  pl.BlockSpec((None, tf, D_MODEL), lambda b, i, f: (layer, nf - 1, 0))] + extra_specs,
        out_specs=[pl.BlockSpec((nb, tt, D_MODEL), lambda b, i, f: (b, i, 0)),
                   pl.BlockSpec((nb, 1, SUBLANES, tf), lambda b, i, f: (b, i, 0, f))],
        out_shape=[jax.ShapeDtypeStruct((bsz, t, D_MODEL), F32),
                   jax.ShapeDtypeStruct((bsz, nt, SUBLANES, D_FF), F32)],
        scratch_shapes=[pltpu.VMEM((nb * (hx + tt), D_MODEL), BF16),
                        pltpu.VMEM((nb, hl + tt, tf), F32),
                        pltpu.VMEM((2, nb * tt, tf), BF16)],
        compiler_params=_params(("parallel", "parallel", "arbitrary")),
    )(x, x_halo, state, g, w_up_all, w_up_all, cw, cb, w_down_all, w_down_all, *extra_args)


def _trunk(x, start_pos, conv_prev, pool_prev, ffn_prev, cache_k, cache_v, page_table, p, cfg):
    bsz, t, _ = x.shape
    m = bsz * t
    nb, tt = cfg["nb"], cfg["tt"]
    convs, pools, ffns = [], [], []
    kv_stacks = ()
    for layer in range(DEPTH):
        li = layer // 2
        g_mix = p["norm_mix"][layer][None, :]
        x2d = x.reshape(m, D_MODEL)
        if layer % 2 == 0:
            glu, b_in = norm_proj(x2d, g_mix, p["w_in_cb"], li, 3, D_A, (F32, F32), _glu_epilogue,
                                  cfg["tm"], cfg["tn"])
            glu = glu.reshape(bsz, t, D_A)
            b_in = b_in.reshape(bsz, t, D_B)
            x = convpool_mix(glu, b_in,
                             _tile_halos(glu, conv_prev[li], tt, CONV_HALO),
                             _tile_halos(b_in, pool_prev[li], tt, POOL_HALO),
                             x, p["conv_w"][li], p["conv_b"][li][None, :], p["ln_a_g"][li][None, :],
                             p["ln_a_b"][li][None, :], p["pool_w"], p["pool_scale"][li][None, :],
                             p["w_out_cb"], li, nb, tt, start_pos)
            convs.append(jnp.concatenate([conv_prev[li], glu], axis=1)[:, -(CONV_W - 1):])
            pools.append(jnp.concatenate([pool_prev[li], b_in], axis=1)[:, -(MAX_WIN - 1):])
        else:
            q_dtype = BF16 if page_table is None else F32
            tn_qkv = cfg["tn"] // 2 if kv_stacks else cfg["tn"]
            q, k_stack, v_stack = norm_proj(x2d, g_mix, p["w_qkv"], li, 3, D_MODEL, (q_dtype, F32, F32),
                                            _qkv_epilogue, cfg["tm"], tn_qkv, prevs=kv_stacks)
            if not kv_stacks:
                k_stack, v_stack = k_stack[None], v_stack[None]
            kv_stacks = (k_stack, v_stack)
            bias2 = p["sb_bias"][li] * LOG2E
            if page_table is None:
                o = sb_prompt_attention(q, k_stack, v_stack, li, bias2, bsz, t, cfg["tq"])
            else:
                o = sb_sample_attention(q, k_stack[li], v_stack[li], bias2, cache_k, cache_v, page_table,
                                        li, bsz, t, cfg["pages_per_step"])
            x = proj_res(o, p["w_o_sb"], li, x2d, cfg["tm"], cfg["tn"]).reshape(bsz, t, D_MODEL)
        hl = cfg["ffn_halo"]
        if cfg["ffn_recompute"]:
            x_halo = _tile_halos(x, jnp.zeros((bsz, 0, D_MODEL), F32), cfg["ffn_tt"], hl)
        else:
            x_halo = jnp.zeros((bsz, 1, SUBLANES, D_MODEL), F32)
        state = jnp.pad(ffn_prev[layer], ((0, 0), (hl - (FFN_CONV_W - 1), 0), (0, 0)))
        final_g = p["norm_final"][None, :] if layer == DEPTH - 1 else None
        x, a_tail = conv_ffn(x, x_halo, state, p["norm_ffn"][layer][None, :], p["w_up"],
                             p["ffn_conv_w"][layer], p["ffn_conv_b"][layer][None, :], p["w_down"],
                             layer, cfg["ffn_nb"], cfg["ffn_tt"], cfg["tf"], hl, cfg["ffn_recompute"],
                             final_g)
        ffns.append(a_tail[:, -1, -(FFN_CONV_W - 1):, :])
    k_stack, v_stack = (a.reshape(a.shape[0], bsz, t, N_HEADS, HEAD_DIM) for a in kv_stacks)
    return x, jnp.stack(convs), jnp.stack(pools), k_stack, v_stack, jnp.stack(ffns)


PROMPT_CFG = dict(nb=1, tt=256, tm=1024, tn=512, tq=512, ffn_nb=1, ffn_tt=512, tf=512, ffn_halo=16,
                  ffn_recompute=True)
SAMPLE_CFG = dict(nb=8, tt=8, tm=64, tn=512, pages_per_step=8, ffn_nb=8, ffn_tt=8, tf=512, ffn_halo=8,
                  ffn_recompute=False)


def _chunk_columns(w, tn):
    layers, k, n = w.shape
    return w.reshape(layers, k, n // tn, tn).transpose(0, 2, 1, 3)


def kernel(x_prompt, x_sample, state_conv, state_pool, cache_k, cache_v, page_table, state_ffn, norm_mix, norm_ffn, norm_final, w_in_cb, conv_w, conv_b, ln_a_g, ln_a_b, pool_w, pool_scale, w_out_cb, w_qkv, w_o_sb, sb_bias, w_up, ffn_conv_w, ffn_conv_b, w_down):
    bp = x_prompt.shape[0]
    p = dict(norm_mix=norm_mix, norm_ffn=norm_ffn, norm_final=norm_final,
             w_in_cb=w_in_cb.astype(BF16), conv_w=conv_w, conv_b=conv_b, ln_a_g=ln_a_g, ln_a_b=ln_a_b,
             pool_w=pool_w.astype(BF16), pool_scale=pool_scale, w_out_cb=w_out_cb.astype(BF16),
             w_qkv=w_qkv.astype(BF16), w_o_sb=w_o_sb.astype(BF16), sb_bias=sb_bias,
             w_up=_chunk_columns(w_up, PROMPT_CFG["tf"]).astype(BF16), ffn_conv_w=ffn_conv_w,
             ffn_conv_b=ffn_conv_b,
             w_down=w_down.astype(BF16))
    n_cb, n_ffn = state_conv.shape[0], state_ffn.shape[0]
    zero_conv = jnp.zeros((n_cb, bp, CONV_W - 1, D_A), F32)
    zero_pool = jnp.zeros((n_cb, bp, MAX_WIN - 1, D_B), F32)
    zero_ffn = jnp.zeros((n_ffn, bp, FFN_CONV_W - 1, D_FF), F32)
    out_p = _trunk(x_prompt, 0, zero_conv, zero_pool, zero_ffn, None, None, None, p, PROMPT_CFG)
    past_len = page_table.shape[1] * PAGE_SIZE
    out_s = _trunk(x_sample, past_len, state_conv, state_pool, state_ffn, cache_k, cache_v,
                   page_table, p, SAMPLE_CFG)
    return (out_p[0], out_s[0]) + out_p[1:] + out_s[1:]
```
